```python
import jax, jax.numpy as jnp
from jax import lax
import numpy as np

D_MODEL = 1024
BATCH = 8
SEQ = 2048
DEPTH = 1

CONV_WIDTH = D_MODEL
CONV_GROUPS = 8
LRU_WIDTH = D_MODEL
LRU_HEADS = 4
LRU_HEAD_DIM = LRU_WIDTH // LRU_HEADS
SHORT_CONV_K = 3
LRU_CONV_K = 4
FFN_CONV_K = 3
D_FF = 3 * D_MODEL
LRU_C = 8.0
RMS_EPS = 1e-6
IN_COLS = 3 * CONV_WIDTH + 2 * LRU_WIDTH + 2 * D_MODEL
SPLITS = (CONV_WIDTH, 2 * CONV_WIDTH, 3 * CONV_WIDTH,
          3 * CONV_WIDTH + LRU_WIDTH, 3 * CONV_WIDTH + 2 * LRU_WIDTH,
          3 * CONV_WIDTH + 2 * LRU_WIDTH + D_MODEL)

kernel_name = "hybrid_shortconv_rglru_convffn_sandwich"


def rmsnorm(x, g):
    xf = x.astype(jnp.float32)
    y = xf * lax.rsqrt(jnp.mean(xf * xf, axis=-1, keepdims=True) + RMS_EPS)
    return (y * g.astype(jnp.float32)).astype(x.dtype)


def causal_dwconv(x, w, b=None):
    k_width = w.shape[0]
    s = x.shape[1]
    xp = jnp.pad(x, ((0, 0), (k_width - 1, 0), (0, 0)))
    y = xp[:, 0:s] * w[0]
    for k in range(1, k_width):
        y = y + xp[:, k:k + s] * w[k]
    if b is not None:
        y = y + b
    return y


def rg_lru(x, w_a, b_a, w_x, b_x, lam):
    bsz, s, w = x.shape
    xf = x.astype(jnp.float32)
    xh = xf.reshape(bsz, s, LRU_HEADS, LRU_HEAD_DIM)
    r = jax.nn.sigmoid(jnp.einsum("bshi,hij->bshj", xh, w_a.astype(jnp.float32)) + b_a.astype(jnp.float32)).reshape(bsz, s, w)
    i = jax.nn.sigmoid(jnp.einsum("bshi,hij->bshj", xh, w_x.astype(jnp.float32)) + b_x.astype(jnp.float32)).reshape(bsz, s, w)
    log_a = LRU_C * r * jax.nn.log_sigmoid(lam.astype(jnp.float32))
    a = jnp.exp(log_a)
    mult = jnp.sqrt(-jnp.expm1(2.0 * log_a))
    first = (jnp.arange(s) == 0)[None, :, None]
    mult = jnp.where(first, 1.0, mult)
    u = mult * (i * xf)

    def combine(left, right):
        a1, b1 = left
        a2, b2 = right
        return a1 * a2, a2 * b1 + b2

    _, h = lax.associative_scan(combine, (a, u), axis=1)
    return h.astype(x.dtype)


def setup_inputs(seed: int = 0) -> dict:
    key = jax.random.key(seed)
    ks = jax.random.split(key, 24)
    f32 = jnp.float32

    def nrm(k, shape, fan_in):
        return jax.random.normal(k, shape, f32) * (fan_in ** -0.5)

    def gain(k, n):
        return 1.0 + 0.05 * jax.random.normal(k, (DEPTH, n), f32)

    u = jax.random.uniform(ks[14], (DEPTH, LRU_WIDTH), f32, 0.9, 0.999)
    a0 = u ** (1.0 / LRU_C)
    lam = jnp.log(a0) - jnp.log1p(-a0)

    return {
        "x": jax.random.normal(ks[0], (BATCH, SEQ, D_MODEL), f32),
        "norm_mix_pre": gain(ks[1], D_MODEL),
        "norm_mix_post": gain(ks[2], D_MODEL),
        "norm_ffn_pre": gain(ks[3], D_MODEL),
        "norm_ffn_post": gain(ks[4], D_MODEL),
        "w_in": nrm(ks[5], (DEPTH, D_MODEL, IN_COLS), D_MODEL),
        "conv_short_w": nrm(ks[6], (DEPTH, SHORT_CONV_K, CONV_WIDTH), SHORT_CONV_K),
        "w_conv_branch": nrm(ks[7], (DEPTH, CONV_WIDTH, D_MODEL), CONV_WIDTH),
        "lru_conv_w": nrm(ks[8], (DEPTH, LRU_CONV_K, LRU_WIDTH), LRU_CONV_K),
        "lru_conv_b": 0.02 * jax.random.normal(ks[9], (DEPTH, LRU_WIDTH), f32),
        "lru_wa": nrm(ks[10], (DEPTH, LRU_HEADS, LRU_HEAD_DIM, LRU_HEAD_DIM), LRU_HEAD_DIM),
        "lru_ba": 0.02 * jax.random.normal(ks[11], (DEPTH, LRU_HEADS, LRU_HEAD_DIM), f32),
        "lru_wx": nrm(ks[12], (DEPTH, LRU_HEADS, LRU_HEAD_DIM, LRU_HEAD_DIM), LRU_HEAD_DIM),
        "lru_bx": 0.02 * jax.random.normal(ks[13], (DEPTH, LRU_HEADS, LRU_HEAD_DIM), f32),
        "lru_lambda": lam,
        "w_lru_branch": nrm(ks[15], (DEPTH, LRU_WIDTH, D_MODEL), LRU_WIDTH),
        "w_out": nrm(ks[16], (DEPTH, D_MODEL, D_MODEL), D_MODEL),
        "ffn_w_up": nrm(ks[17], (DEPTH, D_MODEL, 2 * D_FF), D_MODEL),
        "ffn_conv_w": nrm(ks[18], (DEPTH, FFN_CONV_K, 2 * D_FF), FFN_CONV_K),
        "ffn_conv_b": 0.02 * jax.random.normal(ks[19], (DEPTH, 2 * D_FF), f32),
        "ffn_w_down": nrm(ks[20], (DEPTH, D_FF, D_MODEL), D_FF),
    }


def reference(x, norm_mix_pre, norm_mix_post, norm_ffn_pre, norm_ffn_post, w_in, conv_short_w,
              w_conv_branch, lru_conv_w, lru_conv_b, lru_wa, lru_ba, lru_wx, lru_bx, lru_lambda,
              w_lru_branch, w_out, ffn_w_up, ffn_conv_w, ffn_conv_b, ffn_w_down):
    for l in range(DEPTH):
        h = rmsnorm(x, norm_mix_pre[l])
        proj = jnp.einsum("bsd,dc->bsc", h, w_in[l])
        c_b, c_c, c_x, l_x, l_y, g_conv, g_lru = jnp.split(proj, SPLITS, axis=-1)
        y_a = c_b * causal_dwconv(c_c * c_x, conv_short_w[l])
        xl = causal_dwconv(l_x, lru_conv_w[l], lru_conv_b[l])
        hl = rg_lru(xl, lru_wa[l], lru_ba[l], lru_wx[l], lru_bx[l], lru_lambda[l])
        y_b = hl * jax.nn.gelu(l_y, approximate=True)
        merged = (jax.nn.sigmoid(g_conv) * jnp.einsum("bsc,cd->bsd", y_a, w_conv_branch[l])
                  + jax.nn.sigmoid(g_lru) * jnp.einsum("bsc,cd->bsd", y_b, w_lru_branch[l]))
        mix = jnp.einsum("bsd,de->bse", merged, w_out[l])
        x = x + rmsnorm(mix, norm_mix_post[l])
        h = rmsnorm(x, norm_ffn_pre[l])
        up = jnp.einsum("bsd,df->bsf", h, ffn_w_up[l])
        up = causal_dwconv(up, ffn_conv_w[l], ffn_conv_b[l])
        gate, val = jnp.split(up, 2, axis=-1)
        f = jax.nn.gelu(gate, approximate=True) * val
        out = jnp.einsum("bsf,fd->bsd", f, ffn_w_down[l])
        x = x + rmsnorm(out, norm_ffn_post[l])
    return x
```

```python
import functools
import math

import jax
import jax.numpy as jnp
from jax import lax
from jax.experimental import pallas as pl
from jax.experimental.pallas import tpu as pltpu

F32 = jnp.float32
BF16 = jnp.bfloat16

LRU_HEADS = 4
LRU_C = 8.0
RMS_EPS = 1e-6
SUBLANES = 8
SEQ_TILE = 256
FFN_CHUNK = 512
VMEM_LIMIT_BYTES = 56 * 1024 * 1024


def _rmsnorm(x, g):
    ms = jnp.mean(x * x, axis=-1, keepdims=True)
    return x * lax.rsqrt(ms + RMS_EPS) * g


def _gelu_tanh(x):
    c = math.sqrt(2.0 / math.pi)
    return 0.5 * x * (1.0 + jnp.tanh(c * (x + 0.044715 * (x * x * x))))


def _log_sigmoid(x):
    return jnp.minimum(x, 0.0) - jnp.log1p(jnp.exp(-jnp.abs(x)))


def _dot(a, b):
    return jnp.dot(a, b, preferred_element_type=F32)


def _causal_conv_from_buf(buf, cur, w_ref, ts, k_width):
    acc = cur * w_ref[k_width - 1:k_width, :]
    for k in range(k_width - 1):
        off = SUBLANES - (k_width - 1) + k
        acc = acc + buf[off:off + ts, :] * w_ref[k:k + 1, :]
    return acc


def _mixer_kernel(x_ref, g_pre_ref, g_post_ref, w_in_ref, csw_ref, w_cb_ref, lcw_ref, lcb_ref,
                  wa_ref, ba_ref, wx_ref, bx_ref, lam_ref, w_lb_ref, w_out_ref,
                  o_ref, cv_buf, lx_buf, hs_buf, h_carry):
    ts, d = x_ref.shape
    j = pl.program_id(1)

    @pl.when(j == 0)
    def _reset_state():
        cv_buf[0:SUBLANES, :] = jnp.zeros((SUBLANES, d), F32)
        lx_buf[0:SUBLANES, :] = jnp.zeros((SUBLANES, d), F32)
        h_carry[...] = jnp.zeros_like(h_carry)

    x = x_ref[...]
    h = _rmsnorm(x, g_pre_ref[...]).astype(BF16)

    def proj(k):
        return _dot(h, w_in_ref[:, k * d:(k + 1) * d])

    c_b = proj(0)
    v = proj(1) * proj(2)
    cv_buf[SUBLANES:SUBLANES + ts, :] = v
    y_a = c_b * _causal_conv_from_buf(cv_buf, v, csw_ref, ts, csw_ref.shape[0])
    cv_buf[0:SUBLANES, :] = cv_buf[ts:ts + SUBLANES, :]
    br_a = _dot(y_a.astype(BF16), w_cb_ref[...])

    l_x = proj(3)
    lx_buf[SUBLANES:SUBLANES + ts, :] = l_x
    xl = _causal_conv_from_buf(lx_buf, l_x, lcw_ref, ts, lcw_ref.shape[0]) + lcb_ref[...]
    lx_buf[0:SUBLANES, :] = lx_buf[ts:ts + SUBLANES, :]
    xl_b = xl.astype(BF16)
    dh = d // LRU_HEADS

    def gate(w_ref, b_ref):
        z = jnp.concatenate(
            [_dot(xl_b[:, hd * dh:(hd + 1) * dh], w_ref[hd]) for hd in range(LRU_HEADS)], axis=-1)
        return jax.nn.sigmoid(z + b_ref[...])

    r = gate(wa_ref, ba_ref)
    i = gate(wx_ref, bx_ref)
    log_a = r * (LRU_C * _log_sigmoid(lam_ref[...]))
    a = jnp.exp(log_a)
    mult = jnp.sqrt(1.0 - a * a)
    row = lax.broadcasted_iota(jnp.int32, (ts, d), 0)
    seq_start_row = jnp.where(j == 0, 0, -1)
    mult = jnp.where(row == seq_start_row, 1.0, mult)
    u = mult * (i * xl)

    sub = row & (SUBLANES - 1)
    step = 1
    while step < SUBLANES:
        keep = sub >= step
        a_s = jnp.where(keep, pltpu.roll(a, step, 0), 1.0)
        u_s = jnp.where(keep, pltpu.roll(u, step, 0), 0.0)
        u = u + a * u_s
        a = a * a_s
        step *= 2
    h_prev = h_carry[SUBLANES - 1:SUBLANES, :]
    for blk in range(ts // SUBLANES):
        rows = slice(blk * SUBLANES, (blk + 1) * SUBLANES)
        h_blk = a[rows, :] * h_prev + u[rows, :]
        hs_buf[rows, :] = h_blk
        h_prev = h_blk[SUBLANES - 1:SUBLANES, :]
    h_carry[...] = hs_buf[ts - SUBLANES:ts, :]

    y_b = hs_buf[...] * _gelu_tanh(proj(4))
    br_b = _dot(y_b.astype(BF16), w_lb_ref[...])

    merged = jax.nn.sigmoid(proj(5)) * br_a + jax.nn.sigmoid(proj(6)) * br_b
    mix = _dot(merged.astype(BF16), w_out_ref[...])
    o_ref[...] = x + _rmsnorm(mix, g_post_ref[...])


def _ffn_kernel(x_ref, g_pre_ref, g_post_ref, w_up_ref, cw_ref, cb_ref, w_down_ref,
                o_ref, up_buf, f_buf):
    ts, d = x_ref.shape
    d_ff = w_down_ref.shape[0]
    j = pl.program_id(1)

    @pl.when(j == 0)
    def _reset_state():
        up_buf[0:SUBLANES, :] = jnp.zeros((SUBLANES, up_buf.shape[1]), F32)

    x = x_ref[...]
    h = _rmsnorm(x, g_pre_ref[...]).astype(BF16)
    k_width = cw_ref.shape[0]

    def conv_up(col):
        cols = slice(col, col + FFN_CHUNK)
        up = _dot(h, w_up_ref[:, cols])
        up_buf[SUBLANES:SUBLANES + ts, cols] = up
        acc = up * cw_ref[k_width - 1:k_width, cols] + cb_ref[:, cols]
        for k in range(k_width - 1):
            off = SUBLANES - (k_width - 1) + k
            acc = acc + up_buf[off:off + ts, cols] * cw_ref[k:k + 1, cols]
        return acc

    for c in range(d_ff // FFN_CHUNK):
        gate = conv_up(c * FFN_CHUNK)
        val = conv_up(d_ff + c * FFN_CHUNK)
        f_buf[:, c * FFN_CHUNK:(c + 1) * FFN_CHUNK] = (_gelu_tanh(gate) * val).astype(BF16)
    up_buf[0:SUBLANES, :] = up_buf[ts:ts + SUBLANES, :]

    out = _dot(f_buf[...], w_down_ref[...])
    o_ref[...] = x + _rmsnorm(out, g_post_ref[...])


def _full(shape):
    return pl.BlockSpec(shape, lambda b, j: (0,) * len(shape))


def _seq_tiled(d):
    return pl.BlockSpec((None, SEQ_TILE, d), lambda b, j: (b, j, 0))


_COMPILER_PARAMS = pltpu.CompilerParams(
    dimension_semantics=("arbitrary", "arbitrary"), vmem_limit_bytes=VMEM_LIMIT_BYTES)


def _mixer(x, g_pre, g_post, w_in, csw, w_cb, lcw, lcb, wa, ba, wx, bx, lam, w_lb, w_out):
    bsz, s, d = x.shape
    args = (x, g_pre, g_post, w_in, csw, w_cb, lcw, lcb, wa, ba, wx, bx, lam, w_lb, w_out)
    in_specs = [_seq_tiled(d)] + [_full(a.shape) for a in args[1:]]
    return pl.pallas_call(
        _mixer_kernel,
        grid=(bsz, s // SEQ_TILE),
        in_specs=in_specs,
        out_specs=_seq_tiled(d),
        out_shape=jax.ShapeDtypeStruct(x.shape, x.dtype),
        scratch_shapes=[
            pltpu.VMEM((SEQ_TILE + SUBLANES, d), F32),
            pltpu.VMEM((SEQ_TILE + SUBLANES, d), F32),
            pltpu.VMEM((SEQ_TILE, d), F32),
            pltpu.VMEM((SUBLANES, d), F32),
        ],
        compiler_params=_COMPILER_PARAMS,
        name="mixer",
    )(*args)


def _convffn(x, g_pre, g_post, w_up, cw, cb, w_down):
    bsz, s, d = x.shape
    d_ff = w_down.shape[0]
    args = (x, g_pre, g_post, w_up, cw, cb, w_down)
    in_specs = [_seq_tiled(d)] + [_full(a.shape) for a in args[1:]]
    return pl.pallas_call(
        _ffn_kernel,
        grid=(bsz, s // SEQ_TILE),
        in_specs=in_specs,
        out_specs=_seq_tiled(d),
        out_shape=jax.ShapeDtypeStruct(x.shape, x.dtype),
        scratch_shapes=[
            pltpu.VMEM((SEQ_TILE + SUBLANES, 2 * d_ff), F32),
            pltpu.VMEM((SEQ_TILE, d_ff), BF16),
        ],
        compiler_params=_COMPILER_PARAMS,
        name="convffn",
    )(*args)


def kernel(x, norm_mix_pre, norm_mix_post, norm_ffn_pre, norm_ffn_post, w_in, conv_short_w, w_conv_branch, lru_conv_w, lru_conv_b, lru_wa, lru_ba, lru_wx, lru_bx, lru_lambda, w_lru_branch, w_out, ffn_w_up, ffn_conv_w, ffn_conv_b, ffn_w_down):
    depth = w_in.shape[0]
    d = x.shape[-1]
    assert x.shape[1] % SEQ_TILE == 0 and ffn_w_down.shape[1] % FFN_CHUNK == 0
    for l in range(depth):
        x = _mixer(
            x, norm_mix_pre[l][None], norm_mix_post[l][None], w_in[l].astype(BF16),
            conv_short_w[l], w_conv_branch[l].astype(BF16), lru_conv_w[l], lru_conv_b[l][None],
            lru_wa[l].astype(BF16), lru_ba[l].reshape(1, d), lru_wx[l].astype(BF16),
            lru_bx[l].reshape(1, d), lru_lambda[l][None], w_lru_branch[l].astype(BF16),
            w_out[l].astype(BF16))
        x = _convffn(
            x, norm_ffn_pre[l][None], norm_ffn_post[l][None], ffn_w_up[l].astype(BF16),
            ffn_conv_w[l], ffn_conv_b[l][None], ffn_w_down[l].astype(BF16))
    return x
```

```python
import functools
import math

import jax
import jax.numpy as jnp
from jax import lax
from jax.experimental import pallas as pl
from jax.experimental.pallas import tpu as pltpu

F32 = jnp.float32
BF16 = jnp.bfloat16

LRU_HEADS = 4
LRU_C = 8.0
RMS_EPS = 1e-6
SUBLANES = 8
STEPS_PER_TILE = 32
FFN_CHUNK = 512
VMEM_LIMIT_BYTES = 56 * 1024 * 1024


def _rmsnorm(x, g):
    ms = jnp.mean(x * x, axis=-1, keepdims=True)
    return x * lax.rsqrt(ms + RMS_EPS) * g


def _gelu_tanh(x):
    c = math.sqrt(2.0 / math.pi)
    return 0.5 * x * (1.0 + jnp.tanh(c * (x + 0.044715 * (x * x * x))))


def _log_sigmoid(x):
    return jnp.minimum(x, 0.0) - jnp.log1p(jnp.exp(-jnp.abs(x)))


def _pack_weight(w):
    *lead, k, n = w.shape
    pairs = jnp.swapaxes(w.astype(BF16).reshape(*lead, k // 2, 2, n), -1, -2)
    return lax.bitcast_convert_type(pairs, jnp.uint32)


def _dot(a, w_packed):
    w = pltpu.bitcast(w_packed, BF16)
    return jnp.dot(a, w, preferred_element_type=F32)


def _causal_conv(buf, cur, w_ref, cols, rows, batch):
    k_width = w_ref.shape[0]
    acc = cur * w_ref[k_width - 1:k_width, cols]
    for k in range(k_width - 1):
        acc = acc + buf[k * batch:k * batch + rows, cols] * w_ref[k:k + 1, cols]
    return acc


def _mixer_kernel(x_ref, g_pre_ref, g_post_ref, w_in_ref, csw_ref, w_cb_ref, lcw_ref, lcb_ref,
                  wa_ref, ba_ref, wx_ref, bx_ref, lam_ref, w_lb_ref, w_out_ref,
                  o_ref, cv_buf, lx_buf, hs_buf, h_carry, *, batch):
    rows, d = x_ref.shape
    j = pl.program_id(0)
    cv_tail = cv_buf.shape[0] - rows
    lx_tail = lx_buf.shape[0] - rows
    all_cols = slice(0, d)

    @pl.when(j == 0)
    def _reset_state():
        cv_buf[0:cv_tail, :] = jnp.zeros((cv_tail, d), F32)
        lx_buf[0:lx_tail, :] = jnp.zeros((lx_tail, d), F32)
        h_carry[...] = jnp.zeros_like(h_carry)

    x = x_ref[...]
    h = _rmsnorm(x, g_pre_ref[...]).astype(BF16)

    def proj(k):
        return _dot(h, w_in_ref[:, k * d:(k + 1) * d])

    c_b = proj(0)
    v = proj(1) * proj(2)
    cv_buf[cv_tail:cv_tail + rows, :] = v
    y_a = c_b * _causal_conv(cv_buf, v, csw_ref, all_cols, rows, batch)
    cv_buf[0:cv_tail, :] = cv_buf[rows:rows + cv_tail, :]
    br_a = _dot(y_a.astype(BF16), w_cb_ref[...])

    l_x = proj(3)
    lx_buf[lx_tail:lx_tail + rows, :] = l_x
    xl = _causal_conv(lx_buf, l_x, lcw_ref, all_cols, rows, batch) + lcb_ref[...]
    lx_buf[0:lx_tail, :] = lx_buf[rows:rows + lx_tail, :]
    xl_b = xl.astype(BF16)
    dh = d // LRU_HEADS

    def gate(w_ref, b_ref):
        z = jnp.concatenate(
            [_dot(xl_b[:, hd * dh:(hd + 1) * dh], w_ref[hd]) for hd in range(LRU_HEADS)], axis=-1)
        return jax.nn.sigmoid(z + b_ref[...])

    r = gate(wa_ref, ba_ref)
    i = gate(wx_ref, bx_ref)
    log_a = r * (LRU_C * _log_sigmoid(lam_ref[...]))
    a = jnp.exp(log_a)
    mult = jnp.sqrt(1.0 - a * a)
    row = lax.broadcasted_iota(jnp.int32, (rows, d), 0)
    seq_start_rows = jnp.where(j == 0, batch, 0)
    mult = jnp.where(row < seq_start_rows, 1.0, mult)
    u = mult * (i * xl)

    h_t = h_carry[...]
    for t in range(rows // batch):
        rs = slice(t * batch, (t + 1) * batch)
        h_t = a[rs, :] * h_t + u[rs, :]
        hs_buf[rs, :] = h_t
    h_carry[...] = h_t

    y_b = hs_buf[...] * _gelu_tanh(proj(4))
    br_b = _dot(y_b.astype(BF16), w_lb_ref[...])

    merged = jax.nn.sigmoid(proj(5)) * br_a + jax.nn.sigmoid(proj(6)) * br_b
    mix = _dot(merged.astype(BF16), w_out_ref[...])
    o_ref[...] = x + _rmsnorm(mix, g_post_ref[...])


def _ffn_kernel(x_ref, g_pre_ref, g_post_ref, w_up_ref, cw_ref, cb_ref, w_down_ref,
                o_ref, up_buf, f_buf, *, batch):
    rows, d = x_ref.shape
    d_ff = f_buf.shape[1]
    j = pl.program_id(0)
    tail = up_buf.shape[0] - rows

    @pl.when(j == 0)
    def _reset_state():
        up_buf[0:tail, :] = jnp.zeros((tail, up_buf.shape[1]), F32)

    x = x_ref[...]
    h = _rmsnorm(x, g_pre_ref[...]).astype(BF16)

    def conv_up(col):
        cols = slice(col, col + FFN_CHUNK)
        up = _dot(h, w_up_ref[:, cols])
        up_buf[tail:tail + rows, cols] = up
        return _causal_conv(up_buf, up, cw_ref, cols, rows, batch) + cb_ref[:, cols]

    for c in range(d_ff // FFN_CHUNK):
        gate = conv_up(c * FFN_CHUNK)
        val = conv_up(d_ff + c * FFN_CHUNK)
        f_buf[:, c * FFN_CHUNK:(c + 1) * FFN_CHUNK] = (_gelu_tanh(gate) * val).astype(BF16)
    up_buf[0:tail, :] = up_buf[rows:rows + tail, :]

    out = _dot(f_buf[...], w_down_ref[...])
    o_ref[...] = x + _rmsnorm(out, g_post_ref[...])


def _full(shape):
    return pl.BlockSpec(shape, lambda j: (0,) * len(shape))


def _row_tiled(rows, d):
    return pl.BlockSpec((rows, d), lambda j: (j, 0))


_COMPILER_PARAMS = pltpu.CompilerParams(
    dimension_semantics=("arbitrary",), vmem_limit_bytes=VMEM_LIMIT_BYTES)


def _mixer(x, batch, g_pre, g_post, w_in, csw, w_cb, lcw, lcb, wa, ba, wx, bx, lam, w_lb, w_out):
    n, d = x.shape
    rows = STEPS_PER_TILE * batch
    args = (x, g_pre, g_post, w_in, csw, w_cb, lcw, lcb, wa, ba, wx, bx, lam, w_lb, w_out)
    in_specs = [_row_tiled(rows, d)] + [_full(a.shape) for a in args[1:]]
    return pl.pallas_call(
        functools.partial(_mixer_kernel, batch=batch),
        grid=(n // rows,),
        in_specs=in_specs,
        out_specs=_row_tiled(rows, d),
        out_shape=jax.ShapeDtypeStruct(x.shape, x.dtype),
        scratch_shapes=[
            pltpu.VMEM((rows + (csw.shape[0] - 1) * batch, d), F32),
            pltpu.VMEM((rows + (lcw.shape[0] - 1) * batch, d), F32),
            pltpu.VMEM((rows, d), F32),
            pltpu.VMEM((batch, d), F32),
        ],
        compiler_params=_COMPILER_PARAMS,
        name="mixer",
    )(*args)


def _convffn(x, batch, g_pre, g_post, w_up, cw, cb, w_down):
    n, d = x.shape
    d_ff = 2 * w_down.shape[0]
    rows = STEPS_PER_TILE * batch
    args = (x, g_pre, g_post, w_up, cw, cb, w_down)
    in_specs = [_row_tiled(rows, d)] + [_full(a.shape) for a in args[1:]]
    return pl.pallas_call(
        functools.partial(_ffn_kernel, batch=batch),
        grid=(n // rows,),
        in_specs=in_specs,
        out_specs=_row_tiled(rows, d),
        out_shape=jax.ShapeDtypeStruct(x.shape, x.dtype),
        scratch_shapes=[
            pltpu.VMEM((rows + (cw.shape[0] - 1) * batch, 2 * d_ff), F32),
            pltpu.VMEM((rows, d_ff), BF16),
        ],
        compiler_params=_COMPILER_PARAMS,
        name="convffn",
    )(*args)


def kernel(x, norm_mix_pre, norm_mix_post, norm_ffn_pre, norm_ffn_post, w_in, conv_short_w, w_conv_branch, lru_conv_w, lru_conv_b, lru_wa, lru_ba, lru_wx, lru_bx, lru_lambda, w_lru_branch, w_out, ffn_w_up, ffn_conv_w, ffn_conv_b, ffn_w_down):
    depth = w_in.shape[0]
    bsz, s, d = x.shape
    assert bsz == SUBLANES and s % STEPS_PER_TILE == 0 and ffn_w_down.shape[1] % FFN_CHUNK == 0
    xt = x.transpose(1, 0, 2).reshape(s * bsz, d)
    for l in range(depth):
        xt = _mixer(
            xt, bsz, norm_mix_pre[l][None], norm_mix_post[l][None], _pack_weight(w_in[l]),
            conv_short_w[l], _pack_weight(w_conv_branch[l]), lru_conv_w[l], lru_conv_b[l][None],
            _pack_weight(lru_wa[l]), lru_ba[l].reshape(1, d), _pack_weight(lru_wx[l]),
            lru_bx[l].reshape(1, d), lru_lambda[l][None], _pack_weight(w_lru_branch[l]),
            _pack_weight(w_out[l]))
        xt = _convffn(
            xt, bsz, norm_ffn_pre[l][None], norm_ffn_post[l][None], _pack_weight(ffn_w_up[l]),
            ffn_conv_w[l], ffn_conv_b[l][None], _pack_weight(ffn_w_down[l]))
    return xt.reshape(s, bsz, d).transpose(1, 0, 2)
```

```python
import functools
import math

import jax
import jax.numpy as jnp
from jax import lax
from jax.experimental import pallas as pl
from jax.experimental.pallas import tpu as pltpu

F32 = jnp.float32
BF16 = jnp.bfloat16

LRU_HEADS = 4
LRU_C = 8.0
RMS_EPS = 1e-6
SUBLANES = 8
STEPS_PER_TILE = 32
FFN_CHUNK = 512
PACK_BLOCK_BYTES = 4 * 1024 * 1024
VMEM_LIMIT_BYTES = 56 * 1024 * 1024


def _rmsnorm(x, g):
    ms = jnp.mean(x * x, axis=-1, keepdims=True)
    return x * lax.rsqrt(ms + RMS_EPS) * g


def _gelu_tanh(x):
    c = math.sqrt(2.0 / math.pi)
    return 0.5 * x * (1.0 + jnp.tanh(c * (x + 0.044715 * (x * x * x))))


def _log_sigmoid(x):
    return jnp.minimum(x, 0.0) - jnp.log1p(jnp.exp(-jnp.abs(x)))


def _pack_kernel(w_ref, o_ref):
    o_ref[...] = pltpu.bitcast(w_ref[...].astype(BF16), jnp.uint32)


def _pack_weight(w):
    k, n = w.shape
    kb = k
    while kb * n * 4 > PACK_BLOCK_BYTES and kb % 32 == 0:
        kb //= 2
    return pl.pallas_call(
        _pack_kernel,
        grid=(k // kb,),
        in_specs=[pl.BlockSpec((kb, n), lambda i: (i, 0))],
        out_specs=pl.BlockSpec((kb // 2, n), lambda i: (i, 0)),
        out_shape=jax.ShapeDtypeStruct((k // 2, n), jnp.uint32),
        compiler_params=pltpu.CompilerParams(dimension_semantics=("arbitrary",)),
        name="pack_weight",
    )(w)


def _dot(a, w_packed):
    w = pltpu.bitcast(w_packed, BF16)
    return jnp.dot(a, w, preferred_element_type=F32)


def _causal_conv(buf, cur, w_ref, cols, rows, batch):
    k_width = w_ref.shape[0]
    acc = cur * w_ref[k_width - 1:k_width, cols]
    for k in range(k_width - 1):
        acc = acc + buf[k * batch:k * batch + rows, cols] * w_ref[k:k + 1, cols]
    return acc


def _mixer_kernel(x_ref, g_pre_ref, g_post_ref, w_in_ref, csw_ref, w_cb_ref, lcw_ref, lcb_ref,
                  wa_ref, ba_ref, wx_ref, bx_ref, lam_ref, w_lb_ref, w_out_ref,
                  o_ref, cv_buf, lx_buf, hs_buf, h_carry):
    batch, steps, d = x_ref.shape
    rows = steps * batch
    j = pl.program_id(0)
    cv_tail = cv_buf.shape[0] - rows
    lx_tail = lx_buf.shape[0] - rows
    all_cols = slice(0, d)

    @pl.when(j == 0)
    def _reset_state():
        cv_buf[0:cv_tail, :] = jnp.zeros((cv_tail, d), F32)
        lx_buf[0:lx_tail, :] = jnp.zeros((lx_tail, d), F32)
        h_carry[...] = jnp.zeros_like(h_carry)

    x = jnp.concatenate([x_ref[:, t, :] for t in range(steps)], axis=0)
    h = _rmsnorm(x, g_pre_ref[...]).astype(BF16)

    def proj(k):
        return _dot(h, w_in_ref[:, k * d:(k + 1) * d])

    c_b = proj(0)
    v = proj(1) * proj(2)
    cv_buf[cv_tail:cv_tail + rows, :] = v
    y_a = c_b * _causal_conv(cv_buf, v, csw_ref, all_cols, rows, batch)
    cv_buf[0:cv_tail, :] = cv_buf[rows:rows + cv_tail, :]
    br_a = _dot(y_a.astype(BF16), w_cb_ref[...])

    l_x = proj(3)
    lx_buf[lx_tail:lx_tail + rows, :] = l_x
    xl = _causal_conv(lx_buf, l_x, lcw_ref, all_cols, rows, batch) + lcb_ref[...]
    lx_buf[0:lx_tail, :] = lx_buf[rows:rows + lx_tail, :]
    xl_b = xl.astype(BF16)
    dh = d // LRU_HEADS

    def gate(w_ref, b_ref):
        z = jnp.concatenate(
            [_dot(xl_b[:, hd * dh:(hd + 1) * dh], w_ref[hd * dh // 2:(hd + 1) * dh // 2, :])
             for hd in range(LRU_HEADS)], axis=-1)
        return jax.nn.sigmoid(z + b_ref[...])

    r = gate(wa_ref, ba_ref)
    i = gate(wx_ref, bx_ref)
    log_a = r * (LRU_C * _log_sigmoid(lam_ref[...]))
    a = jnp.exp(log_a)
    mult = jnp.sqrt(1.0 - a * a)
    row = lax.broadcasted_iota(jnp.int32, (rows, d), 0)
    seq_start_rows = jnp.where(j == 0, batch, 0)
    mult = jnp.where(row < seq_start_rows, 1.0, mult)
    u = mult * (i * xl)

    h_t = h_carry[...]
    for t in range(rows // batch):
        rs = slice(t * batch, (t + 1) * batch)
        h_t = a[rs, :] * h_t + u[rs, :]
        hs_buf[rs, :] = h_t
    h_carry[...] = h_t

    y_b = hs_buf[...] * _gelu_tanh(proj(4))
    br_b = _dot(y_b.astype(BF16), w_lb_ref[...])

    merged = jax.nn.sigmoid(proj(5)) * br_a + jax.nn.sigmoid(proj(6)) * br_b
    mix = _dot(merged.astype(BF16), w_out_ref[...])
    o_ref[...] = x + _rmsnorm(mix, g_post_ref[...])


def _ffn_kernel(x_ref, g_pre_ref, g_post_ref, w_up_ref, cw_ref, cb_ref, w_down_ref,
                o_ref, up_buf, f_buf):
    rows, d = x_ref.shape
    batch, steps, _ = o_ref.shape
    d_ff = f_buf.shape[1]
    j = pl.program_id(0)
    tail = up_buf.shape[0] - rows

    @pl.when(j == 0)
    def _reset_state():
        up_buf[0:tail, :] = jnp.zeros((tail, up_buf.shape[1]), F32)

    x = x_ref[...]
    h = _rmsnorm(x, g_pre_ref[...]).astype(BF16)

    def conv_up(col):
        cols = slice(col, col + FFN_CHUNK)
        up = _dot(h, w_up_ref[:, cols])
        up_buf[tail:tail + rows, cols] = up
        return _causal_conv(up_buf, up, cw_ref, cols, rows, batch) + cb_ref[:, cols]

    for c in range(d_ff // FFN_CHUNK):
        gate = conv_up(c * FFN_CHUNK)
        val = conv_up(d_ff + c * FFN_CHUNK)
        f_buf[:, c * FFN_CHUNK:(c + 1) * FFN_CHUNK] = (_gelu_tanh(gate) * val).astype(BF16)
    up_buf[0:tail, :] = up_buf[rows:rows + tail, :]

    out = _dot(f_buf[...], w_down_ref[...])
    res = x + _rmsnorm(out, g_post_ref[...])
    for t in range(steps):
        o_ref[:, t, :] = res[t * batch:(t + 1) * batch, :]


def _full(shape):
    return pl.BlockSpec(shape, lambda j: (0,) * len(shape))


def _time_major_tiled(rows, d):
    return pl.BlockSpec((rows, d), lambda j: (j, 0))


def _batch_major_tiled(batch, steps, d):
    return pl.BlockSpec((batch, steps, d), lambda j: (0, j, 0))


_COMPILER_PARAMS = pltpu.CompilerParams(
    dimension_semantics=("arbitrary",), vmem_limit_bytes=VMEM_LIMIT_BYTES)


def _mixer(x, g_pre, g_post, w_in, csw, w_cb, lcw, lcb, wa, ba, wx, bx, lam, w_lb, w_out):
    batch, s, d = x.shape
    rows = STEPS_PER_TILE * batch
    args = (x, g_pre, g_post, w_in, csw, w_cb, lcw, lcb, wa, ba, wx, bx, lam, w_lb, w_out)
    in_specs = [_batch_major_tiled(batch, STEPS_PER_TILE, d)] + [_full(a.shape) for a in args[1:]]
    return pl.pallas_call(
        _mixer_kernel,
        grid=(s // STEPS_PER_TILE,),
        in_specs=in_specs,
        out_specs=_time_major_tiled(rows, d),
        out_shape=jax.ShapeDtypeStruct((s * batch, d), x.dtype),
        scratch_shapes=[
            pltpu.VMEM((rows + (csw.shape[0] - 1) * batch, d), F32),
            pltpu.VMEM((rows + (lcw.shape[0] - 1) * batch, d), F32),
            pltpu.VMEM((rows, d), F32),
            pltpu.VMEM((batch, d), F32),
        ],
        compiler_params=_COMPILER_PARAMS,
        name="mixer",
    )(*args)


def _convffn(x, batch, g_pre, g_post, w_up, cw, cb, w_down):
    n, d = x.shape
    s = n // batch
    d_ff = 2 * w_down.shape[0]
    rows = STEPS_PER_TILE * batch
    args = (x, g_pre, g_post, w_up, cw, cb, w_down)
    in_specs = [_time_major_tiled(rows, d)] + [_full(a.shape) for a in args[1:]]
    return pl.pallas_call(
        _ffn_kernel,
        grid=(s // STEPS_PER_TILE,),
        in_specs=in_specs,
        out_specs=_batch_major_tiled(batch, STEPS_PER_TILE, d),
        out_shape=jax.ShapeDtypeStruct((batch, s, d), x.dtype),
        scratch_shapes=[
            pltpu.VMEM((rows + (cw.shape[0] - 1) * batch, 2 * d_ff), F32),
            pltpu.VMEM((rows, d_ff), BF16),
        ],
        compiler_params=_COMPILER_PARAMS,
        name="convffn",
    )(*args)


def kernel(x, norm_mix_pre, norm_mix_post, norm_ffn_pre, norm_ffn_post, w_in, conv_short_w, w_conv_branch, lru_conv_w, lru_conv_b, lru_wa, lru_ba, lru_wx, lru_bx, lru_lambda, w_lru_branch, w_out, ffn_w_up, ffn_conv_w, ffn_conv_b, ffn_w_down):
    depth = w_in.shape[0]
    bsz, s, d = x.shape
    assert bsz == SUBLANES and s % STEPS_PER_TILE == 0 and ffn_w_down.shape[1] % FFN_CHUNK == 0
    for l in range(depth):
        xt = _mixer(
            x, norm_mix_pre[l][None], norm_mix_post[l][None], _pack_weight(w_in[l]),
            conv_short_w[l], _pack_weight(w_conv_branch[l]), lru_conv_w[l], lru_conv_b[l][None],
            _pack_weight(lru_wa[l].reshape(d, -1)), lru_ba[l].reshape(1, d),
            _pack_weight(lru_wx[l].reshape(d, -1)), lru_bx[l].reshape(1, d),
            lru_lambda[l][None], _pack_weight(w_lru_branch[l]), _pack_weight(w_out[l]))
        x = _convffn(
            xt, bsz, norm_ffn_pre[l][None], norm_ffn_post[l][None], _pack_weight(ffn_w_up[l]),
            ffn_conv_w[l], ffn_conv_b[l][None], _pack_weight(ffn_w_down[l]))
    return x
```

```python
import math

import jax
import jax.numpy as jnp
from jax import lax
from jax.experimental import pallas as pl
from jax.experimental.pallas import tpu as pltpu

F32 = jnp.float32
BF16 = jnp.bfloat16

LRU_HEADS = 4
LRU_C = 8.0
RMS_EPS = 1e-6
SUBLANES = 8
STEPS_PER_TILE = 64
STEPS_PER_CHUNK = 32
FFN_CHUNK = 512
PACK_BLOCK_BYTES = 4 * 1024 * 1024
VMEM_LIMIT_BYTES = 60 * 1024 * 1024


def _rmsnorm(x, g):
    ms = jnp.mean(x * x, axis=-1, keepdims=True)
    return x * lax.rsqrt(ms + RMS_EPS) * g


def _gelu_tanh(x):
    c = math.sqrt(2.0 / math.pi)
    return 0.5 * x * (1.0 + jnp.tanh(c * (x + 0.044715 * (x * x * x))))


def _log_sigmoid(x):
    return jnp.minimum(x, 0.0) - jnp.log1p(jnp.exp(-jnp.abs(x)))


def _pack_kernel(w_ref, o_ref):
    o_ref[...] = pltpu.bitcast(w_ref[...].astype(BF16), jnp.uint32)


def _pack_weight(w):
    k, n = w.shape
    kb = k
    while kb * n * 4 > PACK_BLOCK_BYTES and kb % 32 == 0:
        kb //= 2
    return pl.pallas_call(
        _pack_kernel,
        grid=(k // kb,),
        in_specs=[pl.BlockSpec((kb, n), lambda i: (i, 0))],
        out_specs=pl.BlockSpec((kb // 2, n), lambda i: (i, 0)),
        out_shape=jax.ShapeDtypeStruct((k // 2, n), jnp.uint32),
        compiler_params=pltpu.CompilerParams(dimension_semantics=("arbitrary",)),
        name="pack_weight",
    )(w)


def _dot(a, w_packed):
    w = pltpu.bitcast(w_packed, BF16)
    return jnp.dot(a, w, preferred_element_type=F32)


def _causal_conv(buf, cur, w_ref, cols, row0, nrows, batch):
    k_width = w_ref.shape[0]
    acc = cur * w_ref[k_width - 1:k_width, cols]
    for k in range(k_width - 1):
        start = row0 + k * batch
        acc = acc + buf[start:start + nrows, cols] * w_ref[k:k + 1, cols]
    return acc


def _mixer_kernel(x_ref, g_pre_ref, g_post_ref, w_in_ref, csw_ref, w_cb_ref, lcw_ref, lcb_ref,
                  wa_ref, ba_ref, wx_ref, bx_ref, lam_ref, w_lb_ref, w_out_ref,
                  o_ref, cv_buf, lx_buf, hs_buf, h_carry):
    batch, steps, d = x_ref.shape
    rows = steps * batch
    crows = STEPS_PER_CHUNK * batch
    j = pl.program_id(0)
    cv_tail = cv_buf.shape[0] - rows
    lx_tail = lx_buf.shape[0] - rows
    all_cols = slice(0, d)
    dh = d // LRU_HEADS

    @pl.when(j == 0)
    def _reset_state():
        cv_buf[0:cv_tail, :] = jnp.zeros((cv_tail, d), F32)
        lx_buf[0:lx_tail, :] = jnp.zeros((lx_tail, d), F32)
        h_carry[...] = jnp.zeros_like(h_carry)

    log_a_scale = LRU_C * _log_sigmoid(lam_ref[...])
    h_t = h_carry[...]
    for c in range(steps // STEPS_PER_CHUNK):
        r0 = c * crows
        t0 = c * STEPS_PER_CHUNK
        x = jnp.concatenate([x_ref[:, t0 + t, :] for t in range(STEPS_PER_CHUNK)], axis=0)
        h = _rmsnorm(x, g_pre_ref[...]).astype(BF16)

        def proj(k):
            return _dot(h, w_in_ref[:, k * d:(k + 1) * d])

        c_b = proj(0)
        v = proj(1) * proj(2)
        cv_buf[cv_tail + r0:cv_tail + r0 + crows, :] = v
        y_a = c_b * _causal_conv(cv_buf, v, csw_ref, all_cols, r0, crows, batch)
        br_a = _dot(y_a.astype(BF16), w_cb_ref[...])

        l_x = proj(3)
        lx_buf[lx_tail + r0:lx_tail + r0 + crows, :] = l_x
        xl = _causal_conv(lx_buf, l_x, lcw_ref, all_cols, r0, crows, batch) + lcb_ref[...]
        xl_b = xl.astype(BF16)

        def gate(w_ref, b_ref):
            z = jnp.concatenate(
                [_dot(xl_b[:, hd * dh:(hd + 1) * dh], w_ref[hd * dh // 2:(hd + 1) * dh // 2, :])
                 for hd in range(LRU_HEADS)], axis=-1)
            return jax.nn.sigmoid(z + b_ref[...])

        r = gate(wa_ref, ba_ref)
        i = gate(wx_ref, bx_ref)
        a = jnp.exp(r * log_a_scale)
        mult = jnp.sqrt(1.0 - a * a)
        if c == 0:
            row = lax.broadcasted_iota(jnp.int32, (crows, d), 0)
            seq_start_rows = jnp.where(j == 0, batch, 0)
            mult = jnp.where(row < seq_start_rows, 1.0, mult)
        u = mult * (i * xl)

        for t in range(STEPS_PER_CHUNK):
            rs = slice(t * batch, (t + 1) * batch)
            h_t = a[rs, :] * h_t + u[rs, :]
            hs_buf[r0 + t * batch:r0 + (t + 1) * batch, :] = h_t

        y_b = hs_buf[r0:r0 + crows, :] * _gelu_tanh(proj(4))
        br_b = _dot(y_b.astype(BF16), w_lb_ref[...])

        merged = jax.nn.sigmoid(proj(5)) * br_a + jax.nn.sigmoid(proj(6)) * br_b
        mix = _dot(merged.astype(BF16), w_out_ref[...])
        o_ref[r0:r0 + crows, :] = x + _rmsnorm(mix, g_post_ref[...])

    h_carry[...] = h_t
    cv_buf[0:cv_tail, :] = cv_buf[rows:rows + cv_tail, :]
    lx_buf[0:lx_tail, :] = lx_buf[rows:rows + lx_tail, :]


def _ffn_kernel(x_ref, g_pre_ref, g_post_ref, w_up_ref, cw_ref, cb_ref, w_down_ref,
                o_ref, up_buf, f_buf):
    rows, d = x_ref.shape
    batch, steps, _ = o_ref.shape
    crows = STEPS_PER_CHUNK * batch
    d_ff = f_buf.shape[1]
    j = pl.program_id(0)
    tail = up_buf.shape[0] - rows

    @pl.when(j == 0)
    def _reset_state():
        up_buf[0:tail, :] = jnp.zeros((tail, up_buf.shape[1]), F32)

    for c in range(steps // STEPS_PER_CHUNK):
        r0 = c * crows
        t0 = c * STEPS_PER_CHUNK
        x = x_ref[r0:r0 + crows, :]
        h = _rmsnorm(x, g_pre_ref[...]).astype(BF16)

        def conv_up(col):
            cols = slice(col, col + FFN_CHUNK)
            up = _dot(h, w_up_ref[:, cols])
            up_buf[tail + r0:tail + r0 + crows, cols] = up
            return _causal_conv(up_buf, up, cw_ref, cols, r0, crows, batch) + cb_ref[:, cols]

        for n in range(d_ff // FFN_CHUNK):
            gate = conv_up(n * FFN_CHUNK)
            val = conv_up(d_ff + n * FFN_CHUNK)
            f_buf[r0:r0 + crows, n * FFN_CHUNK:(n + 1) * FFN_CHUNK] = (
                _gelu_tanh(gate) * val).astype(BF16)

        out = _dot(f_buf[r0:r0 + crows, :], w_down_ref[...])
        res = x + _rmsnorm(out, g_post_ref[...])
        for t in range(STEPS_PER_CHUNK):
            o_ref[:, t0 + t, :] = res[t * batch:(t + 1) * batch, :]

    up_buf[0:tail, :] = up_buf[rows:rows + tail, :]


def _full(shape):
    return pl.BlockSpec(shape, lambda j: (0,) * len(shape))


def _time_major_tiled(rows, d):
    return pl.BlockSpec((rows, d), lambda j: (j, 0))


def _batch_major_tiled(batch, steps, d):
    return pl.BlockSpec((batch, steps, d), lambda j: (0, j, 0))


_COMPILER_PARAMS = pltpu.CompilerParams(
    dimension_semantics=("arbitrary",), vmem_limit_bytes=VMEM_LIMIT_BYTES)


def _mixer(x, g_pre, g_post, w_in, csw, w_cb, lcw, lcb, wa, ba, wx, bx, lam, w_lb, w_out):
    batch, s, d = x.shape
    rows = STEPS_PER_TILE * batch
    args = (x, g_pre, g_post, w_in, csw, w_cb, lcw, lcb, wa, ba, wx, bx, lam, w_lb, w_out)
    in_specs = [_batch_major_tiled(batch, STEPS_PER_TILE, d)] + [_full(a.shape) for a in args[1:]]
    return pl.pallas_call(
        _mixer_kernel,
        grid=(s // STEPS_PER_TILE,),
        in_specs=in_specs,
        out_specs=_time_major_tiled(rows, d),
        out_shape=jax.ShapeDtypeStruct((s * batch, d), x.dtype),
        scratch_shapes=[
            pltpu.VMEM((rows + (csw.shape[0] - 1) * batch, d), F32),
            pltpu.VMEM((rows + (lcw.shape[0] - 1) * batch, d), F32),
            pltpu.VMEM((rows, d), F32),
            pltpu.VMEM((batch, d), F32),
        ],
        compiler_params=_COMPILER_PARAMS,
        name="mixer",
    )(*args)


def _convffn(x, batch, g_pre, g_post, w_up, cw, cb, w_down):
    n, d = x.shape
    s = n // batch
    d_ff = 2 * w_down.shape[0]
    rows = STEPS_PER_TILE * batch
    args = (x, g_pre, g_post, w_up, cw, cb, w_down)
    in_specs = [_time_major_tiled(rows, d)] + [_full(a.shape) for a in args[1:]]
    return pl.pallas_call(
        _ffn_kernel,
        grid=(s // STEPS_PER_TILE,),
        in_specs=in_specs,
        out_specs=_batch_major_tiled(batch, STEPS_PER_TILE, d),
        out_shape=jax.ShapeDtypeStruct((batch, s, d), x.dtype),
        scratch_shapes=[
            pltpu.VMEM((rows + (cw.shape[0] - 1) * batch, 2 * d_ff), F32),
            pltpu.VMEM((rows, d_ff), BF16),
        ],
        compiler_params=_COMPILER_PARAMS,
        name="convffn",
    )(*args)


def kernel(x, norm_mix_pre, norm_mix_post, norm_ffn_pre, norm_ffn_post, w_in, conv_short_w, w_conv_branch, lru_conv_w, lru_conv_b, lru_wa, lru_ba, lru_wx, lru_bx, lru_lambda, w_lru_branch, w_out, ffn_w_up, ffn_conv_w, ffn_conv_b, ffn_w_down):
    depth = w_in.shape[0]
    bsz, s, d = x.shape
    assert bsz == SUBLANES and s % STEPS_PER_TILE == 0 and STEPS_PER_TILE % STEPS_PER_CHUNK == 0
    assert ffn_w_down.shape[1] % FFN_CHUNK == 0
    for l in range(depth):
        xt = _mixer(
            x, norm_mix_pre[l][None], norm_mix_post[l][None], _pack_weight(w_in[l]),
            conv_short_w[l], _pack_weight(w_conv_branch[l]), lru_conv_w[l], lru_conv_b[l][None],
            _pack_weight(lru_wa[l].reshape(d, -1)), lru_ba[l].reshape(1, d),
            _pack_weight(lru_wx[l].reshape(d, -1)), lru_bx[l].reshape(1, d),
            lru_lambda[l][None], _pack_weight(w_lru_branch[l]), _pack_weight(w_out[l]))
        x = _convffn(
            xt, bsz, norm_ffn_pre[l][None], norm_ffn_post[l][None], _pack_weight(ffn_w_up[l]),
            ffn_conv_w[l], ffn_conv_b[l][None], _pack_weight(ffn_w_down[l]))
    return x
```

```python
import math

import jax
import jax.numpy as jnp
from jax import lax
from jax.experimental import pallas as pl
from jax.experimental.pallas import tpu as pltpu

F32 = jnp.float32
BF16 = jnp.bfloat16

LRU_HEADS = 4
LRU_C = 8.0
RMS_EPS = 1e-6
SUBLANES = 8
STEPS_PER_TILE = 32
FFN_CHUNK = 512
X_RING = 3
PACK_BLOCK_BYTES = 4 * 1024 * 1024
VMEM_LIMIT_BYTES = 56 * 1024 * 1024


def _rmsnorm(x, g):
    ms = jnp.mean(x * x, axis=-1, keepdims=True)
    return x * lax.rsqrt(ms + RMS_EPS) * g


def _gelu_tanh(x):
    c = math.sqrt(2.0 / math.pi)
    return 0.5 * x * (1.0 + jnp.tanh(c * (x + 0.044715 * (x * x * x))))


def _log_sigmoid(x):
    return jnp.minimum(x, 0.0) - jnp.log1p(jnp.exp(-jnp.abs(x)))


def _pack_kernel(w_ref, o_ref):
    o_ref[...] = pltpu.bitcast(w_ref[...].astype(BF16), jnp.uint32)


def _pack_weight(w):
    k, n = w.shape
    kb = k
    while kb * n * 4 > PACK_BLOCK_BYTES and kb % 32 == 0:
        kb //= 2
    return pl.pallas_call(
        _pack_kernel,
        grid=(k // kb,),
        in_specs=[pl.BlockSpec((kb, n), lambda i: (i, 0))],
        out_specs=pl.BlockSpec((kb // 2, n), lambda i: (i, 0)),
        out_shape=jax.ShapeDtypeStruct((k // 2, n), jnp.uint32),
        compiler_params=pltpu.CompilerParams(dimension_semantics=("arbitrary",)),
        name="pack_weight",
    )(w)


def _dot(a, w_packed):
    w = pltpu.bitcast(w_packed, BF16)
    return jnp.dot(a, w, preferred_element_type=F32)


def _causal_conv(buf, cur, w_ref, cols, nrows, batch):
    k_width = w_ref.shape[0]
    acc = cur * w_ref[k_width - 1:k_width, cols]
    for k in range(k_width - 1):
        acc = acc + buf[k * batch:k * batch + nrows, cols] * w_ref[k:k + 1, cols]
    return acc


def _mixer_kernel(x_first_ref, x_next_ref, g_pre_ref, g_post_ref, w_in_ref, csw_ref, w_cb_ref,
                  lcw_ref, lcb_ref, wa_ref, ba_ref, wx_ref, bx_ref, lam_ref, w_lb_ref, w_out_ref,
                  o_ref, cv_buf, lx_buf, hs_buf, h_carry, xs_buf, h_buf, merged_buf):
    batch, steps, d = x_next_ref.shape
    rows = steps * batch
    j = pl.program_id(0)
    n_tiles = pl.num_programs(0) - 1
    cv_tail = cv_buf.shape[0] - rows
    lx_tail = lx_buf.shape[0] - rows
    all_cols = slice(0, d)
    dh = d // LRU_HEADS

    def stage_input(x_ref, slot):
        x = jnp.concatenate([x_ref[:, t, :] for t in range(steps)], axis=0)
        xs_buf[slot] = x
        h_buf[...] = _rmsnorm(x, g_pre_ref[...]).astype(BF16)

    def finish_previous_tile():
        mix = _dot(merged_buf[...], w_out_ref[...])
        o_ref[...] = xs_buf[lax.rem(j + X_RING - 1, X_RING)] + _rmsnorm(mix, g_post_ref[...])

    @pl.when(j == 0)
    def _start():
        cv_buf[0:cv_tail, :] = jnp.zeros((cv_tail, d), F32)
        lx_buf[0:lx_tail, :] = jnp.zeros((lx_tail, d), F32)
        h_carry[...] = jnp.zeros_like(h_carry)
        merged_buf[...] = jnp.zeros_like(merged_buf)
        xs_buf[X_RING - 1] = jnp.zeros((rows, d), F32)
        stage_input(x_first_ref, 0)

    @pl.when(j < n_tiles)
    def _tile():
        finish_previous_tile()

        def proj(k):
            return _dot(h_buf[...], w_in_ref[:, k * d:(k + 1) * d])

        l_x = proj(3)
        lx_buf[lx_tail:lx_tail + rows, :] = l_x
        xl = _causal_conv(lx_buf, l_x, lcw_ref, all_cols, rows, batch) + lcb_ref[...]
        lx_buf[0:lx_tail, :] = lx_buf[rows:rows + lx_tail, :]
        xl_b = xl.astype(BF16)

        v = proj(1) * proj(2)
        cv_buf[cv_tail:cv_tail + rows, :] = v
        conv_v = _causal_conv(cv_buf, v, csw_ref, all_cols, rows, batch)
        cv_buf[0:cv_tail, :] = cv_buf[rows:rows + cv_tail, :]

        def gate(w_ref, b_ref):
            z = jnp.concatenate(
                [_dot(xl_b[:, hd * dh:(hd + 1) * dh], w_ref[hd * dh // 2:(hd + 1) * dh // 2, :])
                 for hd in range(LRU_HEADS)], axis=-1)
            return jax.nn.sigmoid(z + b_ref[...])

        r = gate(wa_ref, ba_ref)
        i = gate(wx_ref, bx_ref)
        y_a = proj(0) * conv_v

        a = jnp.exp(r * (LRU_C * _log_sigmoid(lam_ref[...])))
        mult = jnp.sqrt(1.0 - a * a)
        row = lax.broadcasted_iota(jnp.int32, (rows, d), 0)
        seq_start_rows = jnp.where(j == 0, batch, 0)
        mult = jnp.where(row < seq_start_rows, 1.0, mult)
        u = mult * (i * xl)

        h_t = h_carry[...]
        for t in range(steps):
            rs = slice(t * batch, (t + 1) * batch)
            h_t = a[rs, :] * h_t + u[rs, :]
            hs_buf[rs, :] = h_t
        h_carry[...] = h_t

        gelu_y = _gelu_tanh(proj(4))
        sig_conv = jax.nn.sigmoid(proj(5))
        sig_lru = jax.nn.sigmoid(proj(6))

        stage_input(x_next_ref, lax.rem(j + 1, X_RING))

        br_a = _dot(y_a.astype(BF16), w_cb_ref[...])
        br_b = _dot((hs_buf[...] * gelu_y).astype(BF16), w_lb_ref[...])
        merged_buf[...] = (sig_conv * br_a + sig_lru * br_b).astype(BF16)

    @pl.when(j == n_tiles)
    def _drain():
        finish_previous_tile()


def _ffn_kernel(x_prev_ref, x_next_ref, g_pre_ref, g_post_ref, w_up_ref, cw_ref, cb_ref, w_down_ref,
                o_ref, up_buf, f_buf, h_buf, raw_buf):
    rows, d = x_prev_ref.shape
    batch, steps, _ = o_ref.shape
    d_ff = f_buf.shape[1]
    j = pl.program_id(0)
    n_tiles = pl.num_programs(0) - 1
    tail = up_buf.shape[0] - rows

    def stage_input(x_ref):
        h_buf[...] = _rmsnorm(x_ref[...], g_pre_ref[...]).astype(BF16)

    def finish_previous_tile():
        res = x_prev_ref[...] + _rmsnorm(raw_buf[...], g_post_ref[...])
        for t in range(steps):
            o_ref[:, t, :] = res[t * batch:(t + 1) * batch, :]

    @pl.when(j == 0)
    def _start():
        up_buf[0:tail, :] = jnp.zeros((tail, up_buf.shape[1]), F32)
        raw_buf[...] = jnp.zeros_like(raw_buf)
        stage_input(x_prev_ref)

    @pl.when(j < n_tiles)
    def _tile():
        finish_previous_tile()

        def conv_up(col):
            cols = slice(col, col + FFN_CHUNK)
            up = _dot(h_buf[...], w_up_ref[:, cols])
            up_buf[tail:tail + rows, cols] = up
            return _causal_conv(up_buf, up, cw_ref, cols, rows, batch) + cb_ref[:, cols]

        for n in range(d_ff // FFN_CHUNK):
            gate = conv_up(n * FFN_CHUNK)
            val = conv_up(d_ff + n * FFN_CHUNK)
            f_buf[:, n * FFN_CHUNK:(n + 1) * FFN_CHUNK] = (_gelu_tanh(gate) * val).astype(BF16)
        up_buf[0:tail, :] = up_buf[rows:rows + tail, :]

        stage_input(x_next_ref)
        raw_buf[...] = _dot(f_buf[...], w_down_ref[...])

    @pl.when(j == n_tiles)
    def _drain():
        finish_previous_tile()


def _full(shape):
    return pl.BlockSpec(shape, lambda j: (0,) * len(shape))


_COMPILER_PARAMS = pltpu.CompilerParams(
    dimension_semantics=("arbitrary",), vmem_limit_bytes=VMEM_LIMIT_BYTES)


def _mixer(x, g_pre, g_post, w_in, csw, w_cb, lcw, lcb, wa, ba, wx, bx, lam, w_lb, w_out):
    batch, s, d = x.shape
    rows = STEPS_PER_TILE * batch
    n_tiles = s // STEPS_PER_TILE
    weights = (g_pre, g_post, w_in, csw, w_cb, lcw, lcb, wa, ba, wx, bx, lam, w_lb, w_out)
    in_specs = [
        pl.BlockSpec((batch, STEPS_PER_TILE, d), lambda j: (0, 0, 0)),
        pl.BlockSpec((batch, STEPS_PER_TILE, d), lambda j: (0, jnp.minimum(j + 1, n_tiles - 1), 0)),
    ] + [_full(a.shape) for a in weights]
    return pl.pallas_call(
        _mixer_kernel,
        grid=(n_tiles + 1,),
        in_specs=in_specs,
        out_specs=pl.BlockSpec((rows, d), lambda j: (jnp.maximum(j - 1, 0), 0)),
        out_shape=jax.ShapeDtypeStruct((s * batch, d), x.dtype),
        scratch_shapes=[
            pltpu.VMEM((rows + (csw.shape[0] - 1) * batch, d), F32),
            pltpu.VMEM((rows + (lcw.shape[0] - 1) * batch, d), F32),
            pltpu.VMEM((rows, d), F32),
            pltpu.VMEM((batch, d), F32),
            pltpu.VMEM((X_RING, rows, d), F32),
            pltpu.VMEM((rows, d), BF16),
            pltpu.VMEM((rows, d), BF16),
        ],
        compiler_params=_COMPILER_PARAMS,
        name="mixer",
    )(x, x, *weights)


def _convffn(x, batch, g_pre, g_post, w_up, cw, cb, w_down):
    n, d = x.shape
    s = n // batch
    d_ff = 2 * w_down.shape[0]
    rows = STEPS_PER_TILE * batch
    n_tiles = s // STEPS_PER_TILE
    weights = (g_pre, g_post, w_up, cw, cb, w_down)
    in_specs = [
        pl.BlockSpec((rows, d), lambda j: (jnp.maximum(j - 1, 0), 0)),
        pl.BlockSpec((rows, d), lambda j: (jnp.minimum(j + 1, n_tiles - 1), 0)),
    ] + [_full(a.shape) for a in weights]
    return pl.pallas_call(
        _ffn_kernel,
        grid=(n_tiles + 1,),
        in_specs=in_specs,
        out_specs=pl.BlockSpec((batch, STEPS_PER_TILE, d), lambda j: (0, jnp.maximum(j - 1, 0), 0)),
        out_shape=jax.ShapeDtypeStruct((batch, s, d), x.dtype),
        scratch_shapes=[
            pltpu.VMEM((rows + (cw.shape[0] - 1) * batch, 2 * d_ff), F32),
            pltpu.VMEM((rows, d_ff), BF16),
            pltpu.VMEM((rows, d), BF16),
            pltpu.VMEM((rows, d), F32),
        ],
        compiler_params=_COMPILER_PARAMS,
        name="convffn",
    )(x, x, *weights)


def kernel(x, norm_mix_pre, norm_mix_post, norm_ffn_pre, norm_ffn_post, w_in, conv_short_w, w_conv_branch, lru_conv_w, lru_conv_b, lru_wa, lru_ba, lru_wx, lru_bx, lru_lambda, w_lru_branch, w_out, ffn_w_up, ffn_conv_w, ffn_conv_b, ffn_w_down):
    depth = w_in.shape[0]
    bsz, s, d = x.shape
    assert bsz == SUBLANES and s % STEPS_PER_TILE == 0 and ffn_w_down.shape[1] % FFN_CHUNK == 0
    for l in range(depth):
        xt = _mixer(
            x, norm_mix_pre[l][None], norm_mix_post[l][None], _pack_weight(w_in[l]),
            conv_short_w[l], _pack_weight(w_conv_branch[l]), lru_conv_w[l], lru_conv_b[l][None],
            _pack_weight(lru_wa[l].reshape(d, -1)), lru_ba[l].reshape(1, d),
            _pack_weight(lru_wx[l].reshape(d, -1)), lru_bx[l].reshape(1, d),
            lru_lambda[l][None], _pack_weight(w_lru_branch[l]), _pack_weight(w_out[l]))
        x = _convffn(
            xt, bsz, norm_ffn_pre[l][None], norm_ffn_post[l][None], _pack_weight(ffn_w_up[l]),
            ffn_conv_w[l], ffn_conv_b[l][None], _pack_weight(ffn_w_down[l]))
    return x
```

```python
import functools
import math

import jax
import jax.numpy as jnp
from jax import lax
from jax.experimental import pallas as pl
from jax.experimental.pallas import tpu as pltpu

F32 = jnp.float32
BF16 = jnp.bfloat16

LRU_HEADS = 4
LRU_C = 8.0
RMS_EPS = 1e-6
SUBLANES = 8
STEPS_PER_TILE = 64
STEPS_PER_CHUNK = 32
FFN_STEPS_PER_TILE = 32
FFN_CHUNK = 512
PACK_BLOCK_BYTES = 4 * 1024 * 1024
VMEM_LIMIT_BYTES = 60 * 1024 * 1024


def _rmsnorm(x, g):
    ms = jnp.mean(x * x, axis=-1, keepdims=True)
    return x * lax.rsqrt(ms + RMS_EPS) * g


def _gelu_tanh(x):
    c = math.sqrt(2.0 / math.pi)
    return 0.5 * x * (1.0 + jnp.tanh(c * (x + 0.044715 * (x * x * x))))


def _log_sigmoid(x):
    return jnp.minimum(x, 0.0) - jnp.log1p(jnp.exp(-jnp.abs(x)))


def _pack_kernel(w_ref, o_ref):
    o_ref[...] = pltpu.bitcast(w_ref[...].astype(BF16), jnp.uint32)


def _pack_weight(w):
    k, n = w.shape
    kb = k
    while kb * n * 4 > PACK_BLOCK_BYTES and kb % 32 == 0:
        kb //= 2
    return pl.pallas_call(
        _pack_kernel,
        grid=(k // kb,),
        in_specs=[pl.BlockSpec((kb, n), lambda i: (i, 0))],
        out_specs=pl.BlockSpec((kb // 2, n), lambda i: (i, 0)),
        out_shape=jax.ShapeDtypeStruct((k // 2, n), jnp.uint32),
        compiler_params=pltpu.CompilerParams(dimension_semantics=("arbitrary",)),
        name="pack_weight",
    )(w)


def _dot(a, w_packed):
    w = pltpu.bitcast(w_packed, BF16)
    return jnp.dot(a, w, preferred_element_type=F32)


def _causal_conv(buf, cur, w_ref, cols, row0, nrows, batch):
    k_width = w_ref.shape[0]
    acc = cur * w_ref[k_width - 1:k_width, cols]
    for k in range(k_width - 1):
        start = row0 + k * batch
        acc = acc + buf[start:start + nrows, cols] * w_ref[k:k + 1, cols]
    return acc


def _ordered_after(guard, x):
    return jnp.where(pl.program_id(0) < 0, guard, x)


def _mixer_kernel(x_ref, g_pre_ref, g_post_ref, w_in_ref, csw_ref, w_cb_ref, lcw_ref, lcb_ref,
                  wa_ref, ba_ref, wx_ref, bx_ref, lam_ref, w_lb_ref, w_out_ref,
                  o_ref, cv_buf, lx_buf, hs_buf, h_carry):
    batch, steps, d = x_ref.shape
    rows = steps * batch
    crows = STEPS_PER_CHUNK * batch
    j = pl.program_id(0)
    cv_tail = cv_buf.shape[0] - rows
    lx_tail = lx_buf.shape[0] - rows
    all_cols = slice(0, d)
    dh = d // LRU_HEADS

    @pl.when(j == 0)
    def _reset_state():
        cv_buf[0:cv_tail, :] = jnp.zeros((cv_tail, d), F32)
        lx_buf[0:lx_tail, :] = jnp.zeros((lx_tail, d), F32)
        h_carry[...] = jnp.zeros_like(h_carry)

    log_a_scale = LRU_C * _log_sigmoid(lam_ref[...])
    h_t = h_carry[...]
    for c in range(steps // STEPS_PER_CHUNK):
        r0 = c * crows
        t0 = c * STEPS_PER_CHUNK
        x = jnp.concatenate([x_ref[:, t0 + t, :] for t in range(STEPS_PER_CHUNK)], axis=0)
        h = _rmsnorm(x, g_pre_ref[...]).astype(BF16)

        def proj(k):
            return _dot(h, w_in_ref[:, k * d:(k + 1) * d])

        c_b = proj(0)
        v = proj(1) * proj(2)
        cv_buf[cv_tail + r0:cv_tail + r0 + crows, :] = v
        y_a = c_b * _causal_conv(cv_buf, v, csw_ref, all_cols, r0, crows, batch)
        br_a = _dot(y_a.astype(BF16), w_cb_ref[...])

        l_x = proj(3)
        lx_buf[lx_tail + r0:lx_tail + r0 + crows, :] = l_x
        xl = _causal_conv(lx_buf, l_x, lcw_ref, all_cols, r0, crows, batch) + lcb_ref[...]
        xl_b = xl.astype(BF16)

        def gate(w_ref, b_ref):
            z = jnp.concatenate(
                [_dot(xl_b[:, hd * dh:(hd + 1) * dh], w_ref[hd * dh // 2:(hd + 1) * dh // 2, :])
                 for hd in range(LRU_HEADS)], axis=-1)
            return jax.nn.sigmoid(z + b_ref[...])

        r = gate(wa_ref, ba_ref)
        i = gate(wx_ref, bx_ref)
        a = jnp.exp(r * log_a_scale)
        mult = jnp.sqrt(1.0 - a * a)
        if c == 0:
            row = lax.broadcasted_iota(jnp.int32, (crows, d), 0)
            seq_start_rows = jnp.where(j == 0, batch, 0)
            mult = jnp.where(row < seq_start_rows, 1.0, mult)
        u = mult * (i * xl)

        for t in range(STEPS_PER_CHUNK):
            rs = slice(t * batch, (t + 1) * batch)
            h_t = a[rs, :] * h_t + u[rs, :]
            hs_buf[r0 + t * batch:r0 + (t + 1) * batch, :] = h_t

        y_b = hs_buf[r0:r0 + crows, :] * _gelu_tanh(proj(4))
        br_b = _dot(y_b.astype(BF16), w_lb_ref[...])

        merged = jax.nn.sigmoid(proj(5)) * br_a + jax.nn.sigmoid(proj(6)) * br_b
        mix = _dot(merged.astype(BF16), w_out_ref[...])
        o_ref[r0:r0 + crows, :] = x + _rmsnorm(mix, g_post_ref[...])

    h_carry[...] = h_t
    cv_buf[0:cv_tail, :] = cv_buf[rows:rows + cv_tail, :]
    lx_buf[0:lx_tail, :] = lx_buf[rows:rows + lx_tail, :]


def _ffn_kernel(x_next_ref, x_prev_ref, g_pre_ref, g_post_ref, w_up_ref, cw_ref, cb_ref, w_down_ref,
                o_ref, up_buf, f_buf, h_buf, raw_buf):
    rows, d = x_next_ref.shape
    batch, steps, _ = o_ref.shape
    d_ff = f_buf.shape[1]
    n_chunks = d_ff // FFN_CHUNK
    j = pl.program_id(0)
    n_tiles = pl.num_programs(0) - 1
    tail = up_buf.shape[0] - rows
    guard_rows = 2 * batch
    guard_cols = 128

    def stage_input(x_ref, slot):
        h_buf[slot] = _rmsnorm(x_ref[...], g_pre_ref[...]).astype(BF16)

    def finish_tile(slot):
        res = x_prev_ref[...] + _rmsnorm(raw_buf[slot], g_post_ref[...])
        for t in range(steps):
            o_ref[:, t, :] = res[t * batch:(t + 1) * batch, :]

    def tile(slot):
        other = 1 - slot
        finish_tile(other)
        stored = o_ref[0, steps - guard_rows:steps, 0:guard_cols].astype(BF16)

        def conv_up(col):
            cols = slice(col, col + FFN_CHUNK)
            up = _dot(h_buf[slot], w_up_ref[:, cols])
            up_buf[tail:tail + rows, cols] = up
            return _causal_conv(up_buf, up, cw_ref, cols, 0, rows, batch) + cb_ref[:, cols]

        for n in range(n_chunks):
            gate = conv_up(n * FFN_CHUNK)
            val = conv_up(d_ff + n * FFN_CHUNK)
            f_buf[:, n * FFN_CHUNK:(n + 1) * FFN_CHUNK] = (_gelu_tanh(gate) * val).astype(BF16)
        up_buf[0:tail, :] = up_buf[rows:rows + tail, :]

        stage_input(x_next_ref, other)
        staged = h_buf[other, rows - guard_rows:rows, 0:guard_cols]

        out = None
        for n in range(n_chunks):
            cols = slice(n * FFN_CHUNK, (n + 1) * FFN_CHUNK)
            guard = {n_chunks // 2: stored, n_chunks - 1: staged}.get(n)
            if guard is not None:
                corner = (slice(0, guard_rows), slice(n * FFN_CHUNK, n * FFN_CHUNK + guard_cols))
                f_buf[corner] = _ordered_after(guard, f_buf[corner])
            part = _dot(f_buf[:, cols], w_down_ref[n * FFN_CHUNK // 2:(n + 1) * FFN_CHUNK // 2, :])
            out = part if out is None else out + part
        raw_buf[slot] = out

    @pl.when(j == 0)
    def _start():
        up_buf[0:tail, :] = jnp.zeros((tail, up_buf.shape[1]), F32)
        raw_buf[1] = jnp.zeros((rows, d), F32)
        stage_input(x_prev_ref, 0)

    for parity in range(2):
        pl.when(jnp.logical_and(j < n_tiles, lax.rem(j, 2) == parity))(
            functools.partial(tile, parity))

    @pl.when(j == n_tiles)
    def _drain():
        finish_tile((n_tiles - 1) % 2)


def _full(shape):
    return pl.BlockSpec(shape, lambda j: (0,) * len(shape))


def _time_major_tiled(rows, d):
    return pl.BlockSpec((rows, d), lambda j: (j, 0))


def _batch_major_tiled(batch, steps, d):
    return pl.BlockSpec((batch, steps, d), lambda j: (0, j, 0))


_COMPILER_PARAMS = pltpu.CompilerParams(
    dimension_semantics=("arbitrary",), vmem_limit_bytes=VMEM_LIMIT_BYTES)


def _mixer(x, g_pre, g_post, w_in, csw, w_cb, lcw, lcb, wa, ba, wx, bx, lam, w_lb, w_out):
    batch, s, d = x.shape
    rows = STEPS_PER_TILE * batch
    args = (x, g_pre, g_post, w_in, csw, w_cb, lcw, lcb, wa, ba, wx, bx, lam, w_lb, w_out)
    in_specs = [_batch_major_tiled(batch, STEPS_PER_TILE, d)] + [_full(a.shape) for a in args[1:]]
    return pl.pallas_call(
        _mixer_kernel,
        grid=(s // STEPS_PER_TILE,),
        in_specs=in_specs,
        out_specs=_time_major_tiled(rows, d),
        out_shape=jax.ShapeDtypeStruct((s * batch, d), x.dtype),
        scratch_shapes=[
            pltpu.VMEM((rows + (csw.shape[0] - 1) * batch, d), F32),
            pltpu.VMEM((rows + (lcw.shape[0] - 1) * batch, d), F32),
            pltpu.VMEM((rows, d), F32),
            pltpu.VMEM((batch, d), F32),
        ],
        compiler_params=_COMPILER_PARAMS,
        name="mixer",
    )(*args)


def _convffn(x, batch, g_pre, g_post, w_up, cw, cb, w_down):
    n, d = x.shape
    s = n // batch
    d_ff = 2 * w_down.shape[0]
    rows = FFN_STEPS_PER_TILE * batch
    n_tiles = s // FFN_STEPS_PER_TILE
    weights = (g_pre, g_post, w_up, cw, cb, w_down)
    in_specs = [
        pl.BlockSpec((rows, d), lambda j: (jnp.minimum(j + 1, n_tiles - 1), 0)),
        pl.BlockSpec((rows, d), lambda j: (jnp.maximum(j - 1, 0), 0)),
    ] + [_full(a.shape) for a in weights]
    return pl.pallas_call(
        _ffn_kernel,
        grid=(n_tiles + 1,),
        in_specs=in_specs,
        out_specs=pl.BlockSpec((batch, FFN_STEPS_PER_TILE, d),
                               lambda j: (0, jnp.maximum(j - 1, 0), 0)),
        out_shape=jax.ShapeDtypeStruct((batch, s, d), x.dtype),
        scratch_shapes=[
            pltpu.VMEM((rows + (cw.shape[0] - 1) * batch, 2 * d_ff), F32),
            pltpu.VMEM((rows, d_ff), BF16),
            pltpu.VMEM((2, rows, d), BF16),
            pltpu.VMEM((2, rows, d), F32),
        ],
        compiler_params=_COMPILER_PARAMS,
        name="convffn",
    )(x, x, *weights)


def kernel(x, norm_mix_pre, norm_mix_post, norm_ffn_pre, norm_ffn_post, w_in, conv_short_w, w_conv_branch, lru_conv_w, lru_conv_b, lru_wa, lru_ba, lru_wx, lru_bx, lru_lambda, w_lru_branch, w_out, ffn_w_up, ffn_conv_w, ffn_conv_b, ffn_w_down):
    depth = w_in.shape[0]
    bsz, s, d = x.shape
    assert bsz == SUBLANES and s % STEPS_PER_TILE == 0 and STEPS_PER_TILE % STEPS_PER_CHUNK == 0
    assert ffn_w_down.shape[1] % FFN_CHUNK == 0
    for l in range(depth):
        xt = _mixer(
            x, norm_mix_pre[l][None], norm_mix_post[l][None], _pack_weight(w_in[l]),
            conv_short_w[l], _pack_weight(w_conv_branch[l]), lru_conv_w[l], lru_conv_b[l][None],
            _pack_weight(lru_wa[l].reshape(d, -1)), lru_ba[l].reshape(1, d),
            _pack_weight(lru_wx[l].reshape(d, -1)), lru_bx[l].reshape(1, d),
            lru_lambda[l][None], _pack_weight(w_lru_branch[l]), _pack_weight(w_out[l]))
        x = _convffn(
            xt, bsz, norm_ffn_pre[l][None], norm_ffn_post[l][None], _pack_weight(ffn_w_up[l]),
            ffn_conv_w[l], ffn_conv_b[l][None], _pack_weight(ffn_w_down[l]))
    return x
```

```python
import math

import jax
import jax.numpy as jnp
from jax import lax
from jax.experimental import pallas as pl
from jax.experimental.pallas import tpu as pltpu

F32 = jnp.float32
BF16 = jnp.bfloat16

LRU_HEADS = 4
LRU_C = 8.0
RMS_EPS = 1e-6
SUBLANES = 8
STEPS_PER_TILE = 64
STEPS_PER_CHUNK = 32
FFN_CHUNK = 512
PACK_STEPS = 16
VMEM_LIMIT_BYTES = 60 * 1024 * 1024


def _rmsnorm(x, g):
    ms = jnp.mean(x * x, axis=-1, keepdims=True)
    return x * lax.rsqrt(ms + RMS_EPS) * g


def _gelu_tanh(x):
    c = math.sqrt(2.0 / math.pi)
    return 0.5 * x * (1.0 + jnp.tanh(c * (x + 0.044715 * (x * x * x))))


def _log_sigmoid(x):
    return jnp.minimum(x, 0.0) - jnp.log1p(jnp.exp(-jnp.abs(x)))


def _pack_rows(step, src_ref, dst_ref):
    kb = src_ref.shape[0] // 2
    start = pl.multiple_of(step * kb, kb)
    dst_ref[pl.ds(start, kb), :] = pltpu.bitcast(src_ref[...].astype(BF16), jnp.uint32)


def _dot(a, w_packed):
    w = pltpu.bitcast(w_packed, BF16)
    return jnp.dot(a, w, preferred_element_type=F32)


def _causal_conv(buf, cur, w_ref, cols, row0, nrows, batch):
    k_width = w_ref.shape[0]
    acc = cur * w_ref[k_width - 1:k_width, cols]
    for k in range(k_width - 1):
        start = row0 + k * batch
        acc = acc + buf[start:start + nrows, cols] * w_ref[k:k + 1, cols]
    return acc


def _mixer_kernel(x_ref, g_pre_ref, g_post_ref, w_in_f32, csw_ref, w_cb_f32, lcw_ref, lcb_ref,
                  wa_f32, ba_ref, wx_f32, bx_ref, lam_ref, w_lb_f32, w_out_f32,
                  o_ref, cv_buf, lx_buf, hs_buf, h_carry,
                  w_in_ref, w_cb_ref, wa_ref, wx_ref, w_lb_ref, w_out_ref):
    batch, steps, d = x_ref.shape
    rows = steps * batch
    crows = STEPS_PER_CHUNK * batch
    j = pl.program_id(0)
    cv_tail = cv_buf.shape[0] - rows
    lx_tail = lx_buf.shape[0] - rows
    all_cols = slice(0, d)
    dh = d // LRU_HEADS

    @pl.when(j == 0)
    def _reset_state():
        cv_buf[0:cv_tail, :] = jnp.zeros((cv_tail, d), F32)
        lx_buf[0:lx_tail, :] = jnp.zeros((lx_tail, d), F32)
        h_carry[...] = jnp.zeros_like(h_carry)

    @pl.when(j < PACK_STEPS)
    def _pack_weights():
        for src, dst in ((w_in_f32, w_in_ref), (w_cb_f32, w_cb_ref), (wa_f32, wa_ref),
                         (wx_f32, wx_ref), (w_lb_f32, w_lb_ref), (w_out_f32, w_out_ref)):
            _pack_rows(j, src, dst)

    @pl.when(j >= PACK_STEPS)
    def _tile():
        log_a_scale = LRU_C * _log_sigmoid(lam_ref[...])
        h_t = h_carry[...]
        for c in range(steps // STEPS_PER_CHUNK):
            r0 = c * crows
            t0 = c * STEPS_PER_CHUNK
            x = jnp.concatenate([x_ref[:, t0 + t, :] for t in range(STEPS_PER_CHUNK)], axis=0)
            h = _rmsnorm(x, g_pre_ref[...]).astype(BF16)

            def proj(k):
                return _dot(h, w_in_ref[:, k * d:(k + 1) * d])

            c_b = proj(0)
            v = proj(1) * proj(2)
            cv_buf[cv_tail + r0:cv_tail + r0 + crows, :] = v
            y_a = c_b * _causal_conv(cv_buf, v, csw_ref, all_cols, r0, crows, batch)
            br_a = _dot(y_a.astype(BF16), w_cb_ref[...])

            l_x = proj(3)
            lx_buf[lx_tail + r0:lx_tail + r0 + crows, :] = l_x
            xl = _causal_conv(lx_buf, l_x, lcw_ref, all_cols, r0, crows, batch) + lcb_ref[...]
            xl_b = xl.astype(BF16)

            def gate(w_ref, b_ref):
                z = jnp.concatenate(
                    [_dot(xl_b[:, hd * dh:(hd + 1) * dh],
                          w_ref[hd * dh // 2:(hd + 1) * dh // 2, :])
                     for hd in range(LRU_HEADS)], axis=-1)
                return jax.nn.sigmoid(z + b_ref[...])

            r = gate(wa_ref, ba_ref)
            i = gate(wx_ref, bx_ref)
            a = jnp.exp(r * log_a_scale)
            mult = jnp.sqrt(1.0 - a * a)
            if c == 0:
                row = lax.broadcasted_iota(jnp.int32, (crows, d), 0)
                seq_start_rows = jnp.where(j == PACK_STEPS, batch, 0)
                mult = jnp.where(row < seq_start_rows, 1.0, mult)
            u = mult * (i * xl)

            for t in range(STEPS_PER_CHUNK):
                rs = slice(t * batch, (t + 1) * batch)
                h_t = a[rs, :] * h_t + u[rs, :]
                hs_buf[r0 + t * batch:r0 + (t + 1) * batch, :] = h_t

            y_b = hs_buf[r0:r0 + crows, :] * _gelu_tanh(proj(4))
            br_b = _dot(y_b.astype(BF16), w_lb_ref[...])

            merged = jax.nn.sigmoid(proj(5)) * br_a + jax.nn.sigmoid(proj(6)) * br_b
            mix = _dot(merged.astype(BF16), w_out_ref[...])
            o_ref[r0:r0 + crows, :] = x + _rmsnorm(mix, g_post_ref[...])

        h_carry[...] = h_t
        cv_buf[0:cv_tail, :] = cv_buf[rows:rows + cv_tail, :]
        lx_buf[0:lx_tail, :] = lx_buf[rows:rows + lx_tail, :]


def _ffn_kernel(x_ref, g_pre_ref, g_post_ref, w_up_f32, cw_ref, cb_ref, w_down_f32,
                o_ref, up_buf, f_buf, w_up_ref, w_down_ref):
    rows, d = x_ref.shape
    batch, steps, _ = o_ref.shape
    crows = STEPS_PER_CHUNK * batch
    d_ff = f_buf.shape[1]
    j = pl.program_id(0)
    tail = up_buf.shape[0] - rows

    @pl.when(j == 0)
    def _reset_state():
        up_buf[0:tail, :] = jnp.zeros((tail, up_buf.shape[1]), F32)

    @pl.when(j < PACK_STEPS)
    def _pack_weights():
        _pack_rows(j, w_up_f32, w_up_ref)
        _pack_rows(j, w_down_f32, w_down_ref)

    @pl.when(j >= PACK_STEPS)
    def _tile():
        for c in range(steps // STEPS_PER_CHUNK):
            r0 = c * crows
            t0 = c * STEPS_PER_CHUNK
            x = x_ref[r0:r0 + crows, :]
            h = _rmsnorm(x, g_pre_ref[...]).astype(BF16)

            def conv_up(col):
                cols = slice(col, col + FFN_CHUNK)
                up = _dot(h, w_up_ref[:, cols])
                up_buf[tail + r0:tail + r0 + crows, cols] = up
                return _causal_conv(up_buf, up, cw_ref, cols, r0, crows, batch) + cb_ref[:, cols]

            for n in range(d_ff // FFN_CHUNK):
                gate = conv_up(n * FFN_CHUNK)
                val = conv_up(d_ff + n * FFN_CHUNK)
                f_buf[r0:r0 + crows, n * FFN_CHUNK:(n + 1) * FFN_CHUNK] = (
                    _gelu_tanh(gate) * val).astype(BF16)

            out = _dot(f_buf[r0:r0 + crows, :], w_down_ref[...])
            res = x + _rmsnorm(out, g_post_ref[...])
            for t in range(STEPS_PER_CHUNK):
                o_ref[:, t0 + t, :] = res[t * batch:(t + 1) * batch, :]

        up_buf[0:tail, :] = up_buf[rows:rows + tail, :]


def _full(a):
    return pl.BlockSpec(a.shape, lambda j: (0,) * a.ndim)


def _weight_rows(w):
    return pl.BlockSpec((w.shape[0] // PACK_STEPS, w.shape[1]),
                        lambda j: (jnp.minimum(j, PACK_STEPS - 1), 0))


def _packed(w):
    return pltpu.VMEM((w.shape[0] // 2, w.shape[1]), jnp.uint32)


def _tile_index(j):
    return jnp.maximum(j - PACK_STEPS, 0)


_COMPILER_PARAMS = pltpu.CompilerParams(
    dimension_semantics=("arbitrary",), vmem_limit_bytes=VMEM_LIMIT_BYTES)


def _mixer(x, g_pre, g_post, w_in, csw, w_cb, lcw, lcb, wa, ba, wx, bx, lam, w_lb, w_out):
    batch, s, d = x.shape
    rows = STEPS_PER_TILE * batch
    matrices = (w_in, w_cb, wa, wx, w_lb, w_out)
    args = (g_pre, g_post, w_in, csw, w_cb, lcw, lcb, wa, ba, wx, bx, lam, w_lb, w_out)
    return pl.pallas_call(
        _mixer_kernel,
        grid=(PACK_STEPS + s // STEPS_PER_TILE,),
        in_specs=[pl.BlockSpec((batch, STEPS_PER_TILE, d), lambda j: (0, _tile_index(j), 0)),
                  _full(g_pre), _full(g_post), _weight_rows(w_in), _full(csw), _weight_rows(w_cb),
                  _full(lcw), _full(lcb), _weight_rows(wa), _full(ba), _weight_rows(wx),
                  _full(bx), _full(lam), _weight_rows(w_lb), _weight_rows(w_out)],
        out_specs=pl.BlockSpec((rows, d), lambda j: (_tile_index(j), 0)),
        out_shape=jax.ShapeDtypeStruct((s * batch, d), x.dtype),
        scratch_shapes=[
            pltpu.VMEM((rows + (csw.shape[0] - 1) * batch, d), F32),
            pltpu.VMEM((rows + (lcw.shape[0] - 1) * batch, d), F32),
            pltpu.VMEM((rows, d), F32),
            pltpu.VMEM((batch, d), F32),
        ] + [_packed(w) for w in matrices],
        compiler_params=_COMPILER_PARAMS,
        name="mixer",
    )(x, *args)


def _convffn(x, batch, g_pre, g_post, w_up, cw, cb, w_down):
    n, d = x.shape
    s = n // batch
    d_ff = w_down.shape[0]
    rows = STEPS_PER_TILE * batch
    return pl.pallas_call(
        _ffn_kernel,
        grid=(PACK_STEPS + s // STEPS_PER_TILE,),
        in_specs=[pl.BlockSpec((rows, d), lambda j: (_tile_index(j), 0)),
                  _full(g_pre), _full(g_post), _weight_rows(w_up), _full(cw), _full(cb),
                  _weight_rows(w_down)],
        out_specs=pl.BlockSpec((batch, STEPS_PER_TILE, d), lambda j: (0, _tile_index(j), 0)),
        out_shape=jax.ShapeDtypeStruct((batch, s, d), x.dtype),
        scratch_shapes=[
            pltpu.VMEM((rows + (cw.shape[0] - 1) * batch, 2 * d_ff), F32),
            pltpu.VMEM((rows, d_ff), BF16),
            _packed(w_up),
            _packed(w_down),
        ],
        compiler_params=_COMPILER_PARAMS,
        name="convffn",
    )(x, g_pre, g_post, w_up, cw, cb, w_down)


def kernel(x, norm_mix_pre, norm_mix_post, norm_ffn_pre, norm_ffn_post, w_in, conv_short_w, w_conv_branch, lru_conv_w, lru_conv_b, lru_wa, lru_ba, lru_wx, lru_bx, lru_lambda, w_lru_branch, w_out, ffn_w_up, ffn_conv_w, ffn_conv_b, ffn_w_down):
    depth = w_in.shape[0]
    bsz, s, d = x.shape
    assert bsz == SUBLANES and s % STEPS_PER_TILE == 0 and STEPS_PER_TILE % STEPS_PER_CHUNK == 0
    assert ffn_w_down.shape[1] % FFN_CHUNK == 0
    for l in range(depth):
        xt = _mixer(
            x, norm_mix_pre[l][None], norm_mix_post[l][None], w_in[l],
            conv_short_w[l], w_conv_branch[l], lru_conv_w[l], lru_conv_b[l][None],
            lru_wa[l].reshape(d, -1), lru_ba[l].reshape(1, d),
            lru_wx[l].reshape(d, -1), lru_bx[l].reshape(1, d),
            lru_lambda[l][None], w_lru_branch[l], w_out[l])
        x = _convffn(
            xt, bsz, norm_ffn_pre[l][None], norm_ffn_post[l][None], ffn_w_up[l],
            ffn_conv_w[l], ffn_conv_b[l][None], ffn_w_down[l])
    return x
```

```python
import math

import jax
import jax.numpy as jnp
from jax import lax
from jax.experimental import pallas as pl
from jax.experimental.pallas import tpu as pltpu

F32 = jnp.float32
BF16 = jnp.bfloat16

LRU_HEADS = 4
LRU_C = 8.0
RMS_EPS = 1e-6
SUBLANES = 8
STEPS_PER_TILE = 64
STEPS_PER_CHUNK = 32
FFN_CHUNK = 512
PACK_STEPS = 16
VMEM_LIMIT_BYTES = 60 * 1024 * 1024


def _rmsnorm(x, g):
    ms = jnp.mean(x * x, axis=-1, keepdims=True)
    return x * lax.rsqrt(ms + RMS_EPS) * g


def _gelu_tanh(x):
    c = math.sqrt(2.0 / math.pi)
    return 0.5 * x * (1.0 + jnp.tanh(c * (x + 0.044715 * (x * x * x))))


def _log_sigmoid(x):
    return jnp.minimum(x, 0.0) - jnp.log1p(jnp.exp(-jnp.abs(x)))


def _pack_rows(step, src_ref, dst_ref):
    kb = src_ref.shape[0] // 2
    start = pl.multiple_of(step * kb, kb)
    dst_ref[pl.ds(start, kb), :] = pltpu.bitcast(src_ref[...].astype(BF16), jnp.uint32)


def _dot(a, w_packed):
    w = pltpu.bitcast(w_packed, BF16)
    return jnp.dot(a, w, preferred_element_type=F32)


def _causal_conv(buf, cur, w_ref, cols, row0, nrows, batch):
    k_width = w_ref.shape[0]
    acc = cur * w_ref[k_width - 1:k_width, cols]
    for k in range(k_width - 1):
        start = row0 + k * batch
        acc = acc + buf[start:start + nrows, cols] * w_ref[k:k + 1, cols]
    return acc


def _mixer_kernel(x_ref, g_pre_ref, g_post_ref, w_in_f32, csw_ref, w_cb_f32, lcw_ref, lcb_ref,
                  wa_f32, ba_ref, wx_f32, bx_ref, lam_ref, w_lb_f32, w_out_f32,
                  o_ref, cv_buf, lx_buf, hs_buf, h_carry,
                  w_in_ref, w_cb_ref, wa_ref, wx_ref, w_lb_ref, w_out_ref):
    batch, steps, d = x_ref.shape
    rows = steps * batch
    crows = STEPS_PER_CHUNK * batch
    j = pl.program_id(0)
    cv_tail = cv_buf.shape[0] - rows
    lx_tail = lx_buf.shape[0] - rows
    all_cols = slice(0, d)
    dh = d // LRU_HEADS

    @pl.when(j == 0)
    def _reset_state():
        cv_buf[0:cv_tail, :] = jnp.zeros((cv_tail, d), F32)
        lx_buf[0:lx_tail, :] = jnp.zeros((lx_tail, d), F32)
        h_carry[...] = jnp.zeros_like(h_carry)

    @pl.when(j < PACK_STEPS)
    def _pack_weights():
        for src, dst in ((w_in_f32, w_in_ref), (w_cb_f32, w_cb_ref), (wa_f32, wa_ref),
                         (wx_f32, wx_ref), (w_lb_f32, w_lb_ref), (w_out_f32, w_out_ref)):
            _pack_rows(j, src, dst)

    @pl.when(j >= PACK_STEPS)
    def _tile():
        log_a_scale = LRU_C * _log_sigmoid(lam_ref[...])
        n_chunks = steps // STEPS_PER_CHUNK
        state = [{} for _ in range(n_chunks)]

        def projections(c):
            s, t0 = state[c], c * STEPS_PER_CHUNK
            x = jnp.concatenate([x_ref[:, t0 + t, :] for t in range(STEPS_PER_CHUNK)], axis=0)
            h = _rmsnorm(x, g_pre_ref[...]).astype(BF16)
            s["x"] = x
            r0 = c * crows

            def proj(k):
                return _dot(h, w_in_ref[:, k * d:(k + 1) * d])

            l_x = proj(3)
            s["c_c"], s["c_x"] = proj(1), proj(2)
            lx_buf[lx_tail + r0:lx_tail + r0 + crows, :] = l_x
            xl = _causal_conv(lx_buf, l_x, lcw_ref, all_cols, r0, crows, batch) + lcb_ref[...]
            xl_b = xl.astype(BF16)
            z_r, z_i = [], []
            for hd, (name, k) in enumerate((("c_b", 0), ("l_y", 4), ("g_conv", 5), ("g_lru", 6))):
                hrows = slice(hd * dh // 2, (hd + 1) * dh // 2)
                z_r.append(_dot(xl_b[:, hd * dh:(hd + 1) * dh], wa_ref[hrows, :]))
                z_i.append(_dot(xl_b[:, hd * dh:(hd + 1) * dh], wx_ref[hrows, :]))
                s[name] = proj(k)
            s["xl"] = xl
            s["z_r"] = jnp.concatenate(z_r, axis=-1)
            s["z_i"] = jnp.concatenate(z_i, axis=-1)

        def gates_and_conv_branch(c):
            s, r0 = state[c], c * crows
            v = s.pop("c_c") * s.pop("c_x")
            cv_buf[cv_tail + r0:cv_tail + r0 + crows, :] = v
            y_a = s.pop("c_b") * _causal_conv(cv_buf, v, csw_ref, all_cols, r0, crows, batch)
            s["br_a"] = _dot(y_a.astype(BF16), w_cb_ref[...])

        def recurrence_branch(c, h_t):
            s, r0 = state[c], c * crows
            r = jax.nn.sigmoid(s.pop("z_r") + ba_ref[...])
            i = jax.nn.sigmoid(s.pop("z_i") + bx_ref[...])
            a = jnp.exp(r * log_a_scale)
            mult = jnp.sqrt(1.0 - a * a)
            if c == 0:
                row = lax.broadcasted_iota(jnp.int32, (crows, d), 0)
                seq_start_rows = jnp.where(j == PACK_STEPS, batch, 0)
                mult = jnp.where(row < seq_start_rows, 1.0, mult)
            u = mult * (i * s.pop("xl"))
            for t in range(STEPS_PER_CHUNK):
                rs = slice(t * batch, (t + 1) * batch)
                h_t = a[rs, :] * h_t + u[rs, :]
                hs_buf[r0 + t * batch:r0 + (t + 1) * batch, :] = h_t
            y_b = hs_buf[r0:r0 + crows, :] * _gelu_tanh(s.pop("l_y"))
            br_b = _dot(y_b.astype(BF16), w_lb_ref[...])
            merged = (jax.nn.sigmoid(s.pop("g_conv")) * s.pop("br_a")
                      + jax.nn.sigmoid(s.pop("g_lru")) * br_b)
            s["merged"] = merged.astype(BF16)
            return h_t

        def output(c):
            s, r0 = state[c], c * crows
            mix = _dot(s.pop("merged"), w_out_ref[...])
            o_ref[r0:r0 + crows, :] = s.pop("x") + _rmsnorm(mix, g_post_ref[...])

        h_t = h_carry[...]
        projections(0)
        for c in range(n_chunks):
            gates_and_conv_branch(c)
            h_t = recurrence_branch(c, h_t)
            if c + 1 < n_chunks:
                projections(c + 1)
            output(c)

        h_carry[...] = h_t
        cv_buf[0:cv_tail, :] = cv_buf[rows:rows + cv_tail, :]
        lx_buf[0:lx_tail, :] = lx_buf[rows:rows + lx_tail, :]


def _ffn_kernel(x_ref, g_pre_ref, g_post_ref, w_up_f32, cw_ref, cb_ref, w_down_f32,
                o_ref, up_buf, f_buf, w_up_ref, w_down_ref):
    rows, d = x_ref.shape
    batch, steps, _ = o_ref.shape
    crows = STEPS_PER_CHUNK * batch
    d_ff = f_buf.shape[1]
    j = pl.program_id(0)
    tail = up_buf.shape[0] - rows

    @pl.when(j == 0)
    def _reset_state():
        up_buf[0:tail, :] = jnp.zeros((tail, up_buf.shape[1]), F32)

    @pl.when(j < PACK_STEPS)
    def _pack_weights():
        _pack_rows(j, w_up_f32, w_up_ref)
        _pack_rows(j, w_down_f32, w_down_ref)

    @pl.when(j >= PACK_STEPS)
    def _tile():
        for c in range(steps // STEPS_PER_CHUNK):
            r0 = c * crows
            t0 = c * STEPS_PER_CHUNK
            x = x_ref[r0:r0 + crows, :]
            h = _rmsnorm(x, g_pre_ref[...]).astype(BF16)

            def conv_up(col):
                cols = slice(col, col + FFN_CHUNK)
                up = _dot(h, w_up_ref[:, cols])
                up_buf[tail + r0:tail + r0 + crows, cols] = up
                return _causal_conv(up_buf, up, cw_ref, cols, r0, crows, batch) + cb_ref[:, cols]

            for n in range(d_ff // FFN_CHUNK):
                gate = conv_up(n * FFN_CHUNK)
                val = conv_up(d_ff + n * FFN_CHUNK)
                f_buf[r0:r0 + crows, n * FFN_CHUNK:(n + 1) * FFN_CHUNK] = (
                    _gelu_tanh(gate) * val).astype(BF16)

            out = _dot(f_buf[r0:r0 + crows, :], w_down_ref[...])
            res = x + _rmsnorm(out, g_post_ref[...])
            for t in range(STEPS_PER_CHUNK):
                o_ref[:, t0 + t, :] = res[t * batch:(t + 1) * batch, :]

        up_buf[0:tail, :] = up_buf[rows:rows + tail, :]


def _full(a):
    return pl.BlockSpec(a.shape, lambda j: (0,) * a.ndim)


def _weight_rows(w):
    return pl.BlockSpec((w.shape[0] // PACK_STEPS, w.shape[1]),
                        lambda j: (jnp.minimum(j, PACK_STEPS - 1), 0))


def _packed(w):
    return pltpu.VMEM((w.shape[0] // 2, w.shape[1]), jnp.uint32)


def _tile_index(j):
    return jnp.maximum(j - PACK_STEPS, 0)


_COMPILER_PARAMS = pltpu.CompilerParams(
    dimension_semantics=("arbitrary",), vmem_limit_bytes=VMEM_LIMIT_BYTES)


def _mixer(x, g_pre, g_post, w_in, csw, w_cb, lcw, lcb, wa, ba, wx, bx, lam, w_lb, w_out):
    batch, s, d = x.shape
    rows = STEPS_PER_TILE * batch
    matrices = (w_in, w_cb, wa, wx, w_lb, w_out)
    args = (g_pre, g_post, w_in, csw, w_cb, lcw, lcb, wa, ba, wx, bx, lam, w_lb, w_out)
    return pl.pallas_call(
        _mixer_kernel,
        grid=(PACK_STEPS + s // STEPS_PER_TILE,),
        in_specs=[pl.BlockSpec((batch, STEPS_PER_TILE, d), lambda j: (0, _tile_index(j), 0)),
                  _full(g_pre), _full(g_post), _weight_rows(w_in), _full(csw), _weight_rows(w_cb),
                  _full(lcw), _full(lcb), _weight_rows(wa), _full(ba), _weight_rows(wx),
                  _full(bx), _full(lam), _weight_rows(w_lb), _weight_rows(w_out)],
        out_specs=pl.BlockSpec((rows, d), lambda j: (_tile_index(j), 0)),
        out_shape=jax.ShapeDtypeStruct((s * batch, d), x.dtype),
        scratch_shapes=[
            pltpu.VMEM((rows + (csw.shape[0] - 1) * batch, d), F32),
            pltpu.VMEM((rows + (lcw.shape[0] - 1) * batch, d), F32),
            pltpu.VMEM((rows, d), F32),
            pltpu.VMEM((batch, d), F32),
        ] + [_packed(w) for w in matrices],
        compiler_params=_COMPILER_PARAMS,
        name="mixer",
    )(x, *args)


def _convffn(x, batch, g_pre, g_post, w_up, cw, cb, w_down):
    n, d = x.shape
    s = n // batch
    d_ff = w_down.shape[0]
    rows = STEPS_PER_TILE * batch
    return pl.pallas_call(
        _ffn_kernel,
        grid=(PACK_STEPS + s // STEPS_PER_TILE,),
        in_specs=[pl.BlockSpec((rows, d), lambda j: (_tile_index(j), 0)),
                  _full(g_pre), _full(g_post), _weight_rows(w_up), _full(cw), _full(cb),
                  _weight_rows(w_down)],
        out_specs=pl.BlockSpec((batch, STEPS_PER_TILE, d), lambda j: (0, _tile_index(j), 0)),
        out_shape=jax.ShapeDtypeStruct((batch, s, d), x.dtype),
        scratch_shapes=[
            pltpu.VMEM((rows + (cw.shape[0] - 1) * batch, 2 * d_ff), F32),
            pltpu.VMEM((rows, d_ff), BF16),
            _packed(w_up),
            _packed(w_down),
        ],
        compiler_params=_COMPILER_PARAMS,
        name="convffn",
    )(x, g_pre, g_post, w_up, cw, cb, w_down)


def kernel(x, norm_mix_pre, norm_mix_post, norm_ffn_pre, norm_ffn_post, w_in, conv_short_w, w_conv_branch, lru_conv_w, lru_conv_b, lru_wa, lru_ba, lru_wx, lru_bx, lru_lambda, w_lru_branch, w_out, ffn_w_up, ffn_conv_w, ffn_conv_b, ffn_w_down):
    depth = w_in.shape[0]
    bsz, s, d = x.shape
    assert bsz == SUBLANES and s % STEPS_PER_TILE == 0 and STEPS_PER_TILE % STEPS_PER_CHUNK == 0
    assert ffn_w_down.shape[1] % FFN_CHUNK == 0
    for l in range(depth):
        xt = _mixer(
            x, norm_mix_pre[l][None], norm_mix_post[l][None], w_in[l],
            conv_short_w[l], w_conv_branch[l], lru_conv_w[l], lru_conv_b[l][None],
            lru_wa[l].reshape(d, -1), lru_ba[l].reshape(1, d),
            lru_wx[l].reshape(d, -1), lru_bx[l].reshape(1, d),
            lru_lambda[l][None], w_lru_branch[l], w_out[l])
        x = _convffn(
            xt, bsz, norm_ffn_pre[l][None], norm_ffn_post[l][None], ffn_w_up[l],
            ffn_conv_w[l], ffn_conv_b[l][None], ffn_w_down[l])
    return x
```

```python
import math

import jax
import jax.numpy as jnp
from jax import lax
from jax.experimental import pallas as pl
from jax.experimental.pallas import tpu as pltpu

F32 = jnp.float32
BF16 = jnp.bfloat16

LRU_HEADS = 4
LRU_C = 8.0
RMS_EPS = 1e-6
SUBLANES = 8
STEPS_PER_TILE = 64
STEPS_PER_CHUNK = 32
FFN_CHUNK = 512
PACK_STEPS = 16
VMEM_LIMIT_BYTES = 60 * 1024 * 1024


def _rmsnorm(x, g):
    ms = jnp.mean(x * x, axis=-1, keepdims=True)
    return x * lax.rsqrt(ms + RMS_EPS) * g


def _gelu_tanh(x):
    c = math.sqrt(2.0 / math.pi)
    return 0.5 * x * (1.0 + jnp.tanh(c * (x + 0.044715 * (x * x * x))))


def _log_sigmoid(x):
    return jnp.minimum(x, 0.0) - jnp.log1p(jnp.exp(-jnp.abs(x)))


def _pack_rows(step, src_ref, dst_ref):
    kb = src_ref.shape[0] // 2
    start = pl.multiple_of(step * kb, kb)
    dst_ref[pl.ds(start, kb), :] = pltpu.bitcast(src_ref[...].astype(BF16), jnp.uint32)


def _dot(a, w_packed):
    w = pltpu.bitcast(w_packed, BF16)
    return jnp.dot(a, w, preferred_element_type=F32)


def _causal_conv(buf, cur, w_ref, cols, row0, nrows, batch):
    k_width = w_ref.shape[0]
    acc = cur * w_ref[k_width - 1:k_width, cols]
    for k in range(k_width - 1):
        start = row0 + k * batch
        acc = acc + buf[start:start + nrows, cols] * w_ref[k:k + 1, cols]
    return acc


def _mixer_kernel(x_ref, g_pre_ref, g_post_ref, w_in_f32, csw_ref, w_cb_f32, lcw_ref, lcb_ref,
                  wa_f32, ba_ref, wx_f32, bx_ref, lam_ref, w_lb_f32, w_out_f32,
                  o_ref, cv_buf, lx_buf, hs_buf, h_carry,
                  w_in_ref, w_cb_ref, wa_ref, wx_ref, w_lb_ref, w_out_ref):
    batch, steps, d = x_ref.shape
    rows = steps * batch
    crows = STEPS_PER_CHUNK * batch
    j = pl.program_id(0)
    cv_tail = cv_buf.shape[0] - rows
    lx_tail = lx_buf.shape[0] - rows
    all_cols = slice(0, d)
    dh = d // LRU_HEADS

    @pl.when(j == 0)
    def _reset_state():
        cv_buf[0:cv_tail, :] = jnp.zeros((cv_tail, d), F32)
        lx_buf[0:lx_tail, :] = jnp.zeros((lx_tail, d), F32)
        h_carry[...] = jnp.zeros_like(h_carry)

    @pl.when(j < PACK_STEPS)
    def _pack_weights():
        for src, dst in ((w_in_f32, w_in_ref), (w_cb_f32, w_cb_ref), (wa_f32, wa_ref),
                         (wx_f32, wx_ref), (w_lb_f32, w_lb_ref), (w_out_f32, w_out_ref)):
            _pack_rows(j, src, dst)

    @pl.when(j >= PACK_STEPS)
    def _tile():
        log_a_scale = LRU_C * _log_sigmoid(lam_ref[...])
        n_chunks = steps // STEPS_PER_CHUNK
        state = [{} for _ in range(n_chunks)]

        def projections(c):
            s, t0 = state[c], c * STEPS_PER_CHUNK
            x = jnp.swapaxes(x_ref[:, t0:t0 + STEPS_PER_CHUNK, :], 0, 1).reshape(crows, d)
            h = _rmsnorm(x, g_pre_ref[...]).astype(BF16)
            s["x"] = x
            r0 = c * crows

            def proj(k):
                return _dot(h, w_in_ref[:, k * d:(k + 1) * d])

            l_x = proj(3)
            s["c_c"], s["c_x"] = proj(1), proj(2)
            lx_buf[lx_tail + r0:lx_tail + r0 + crows, :] = l_x
            xl = _causal_conv(lx_buf, l_x, lcw_ref, all_cols, r0, crows, batch) + lcb_ref[...]
            xl_b = xl.astype(BF16)
            z_r, z_i = [], []
            for hd, (name, k) in enumerate((("c_b", 0), ("l_y", 4), ("g_conv", 5), ("g_lru", 6))):
                hrows = slice(hd * dh // 2, (hd + 1) * dh // 2)
                z_r.append(_dot(xl_b[:, hd * dh:(hd + 1) * dh], wa_ref[hrows, :]))
                z_i.append(_dot(xl_b[:, hd * dh:(hd + 1) * dh], wx_ref[hrows, :]))
                s[name] = proj(k)
            s["xl"] = xl
            s["z_r"] = jnp.concatenate(z_r, axis=-1)
            s["z_i"] = jnp.concatenate(z_i, axis=-1)

        def gates_and_conv_branch(c):
            s, r0 = state[c], c * crows
            v = s.pop("c_c") * s.pop("c_x")
            cv_buf[cv_tail + r0:cv_tail + r0 + crows, :] = v
            y_a = s.pop("c_b") * _causal_conv(cv_buf, v, csw_ref, all_cols, r0, crows, batch)
            s["br_a"] = _dot(y_a.astype(BF16), w_cb_ref[...])

        def recurrence_branch(c, h_t):
            s, r0 = state[c], c * crows
            r = jax.nn.sigmoid(s.pop("z_r") + ba_ref[...])
            i = jax.nn.sigmoid(s.pop("z_i") + bx_ref[...])
            a = jnp.exp(r * log_a_scale)
            mult = jnp.sqrt(1.0 - a * a)
            if c == 0:
                row = lax.broadcasted_iota(jnp.int32, (crows, d), 0)
                seq_start_rows = jnp.where(j == PACK_STEPS, batch, 0)
                mult = jnp.where(row < seq_start_rows, 1.0, mult)
            u = mult * (i * s.pop("xl"))
            for t in range(STEPS_PER_CHUNK):
                rs = slice(t * batch, (t + 1) * batch)
                h_t = a[rs, :] * h_t + u[rs, :]
                hs_buf[r0 + t * batch:r0 + (t + 1) * batch, :] = h_t
            y_b = hs_buf[r0:r0 + crows, :] * _gelu_tanh(s.pop("l_y"))
            br_b = _dot(y_b.astype(BF16), w_lb_ref[...])
            merged = (jax.nn.sigmoid(s.pop("g_conv")) * s.pop("br_a")
                      + jax.nn.sigmoid(s.pop("g_lru")) * br_b)
            s["merged"] = merged.astype(BF16)
            return h_t

        def output(c):
            s, r0 = state[c], c * crows
            mix = _dot(s.pop("merged"), w_out_ref[...])
            o_ref[r0:r0 + crows, :] = s.pop("x") + _rmsnorm(mix, g_post_ref[...])

        h_t = h_carry[...]
        projections(0)
        for c in range(n_chunks):
            gates_and_conv_branch(c)
            h_t = recurrence_branch(c, h_t)
            if c + 1 < n_chunks:
                projections(c + 1)
            output(c)

        h_carry[...] = h_t
        cv_buf[0:cv_tail, :] = cv_buf[rows:rows + cv_tail, :]
        lx_buf[0:lx_tail, :] = lx_buf[rows:rows + lx_tail, :]


def _ffn_kernel(x_ref, g_pre_ref, g_post_ref, w_up_f32, cw_ref, cb_ref, w_down_f32,
                o_ref, up_buf, f_buf, w_up_ref, w_down_ref):
    rows, d = x_ref.shape
    batch, steps, _ = o_ref.shape
    crows = STEPS_PER_CHUNK * batch
    d_ff = f_buf.shape[1]
    j = pl.program_id(0)
    tail = up_buf.shape[0] - rows

    @pl.when(j == 0)
    def _reset_state():
        up_buf[0:tail, :] = jnp.zeros((tail, up_buf.shape[1]), F32)

    @pl.when(j < PACK_STEPS)
    def _pack_weights():
        _pack_rows(j, w_up_f32, w_up_ref)
        _pack_rows(j, w_down_f32, w_down_ref)

    @pl.when(j >= PACK_STEPS)
    def _tile():
        for c in range(steps // STEPS_PER_CHUNK):
            r0 = c * crows
            t0 = c * STEPS_PER_CHUNK
            x = x_ref[r0:r0 + crows, :]
            h = _rmsnorm(x, g_pre_ref[...]).astype(BF16)

            def conv_up(col):
                cols = slice(col, col + FFN_CHUNK)
                up = _dot(h, w_up_ref[:, cols])
                up_buf[tail + r0:tail + r0 + crows, cols] = up
                return _causal_conv(up_buf, up, cw_ref, cols, r0, crows, batch) + cb_ref[:, cols]

            for n in range(d_ff // FFN_CHUNK):
                gate = conv_up(n * FFN_CHUNK)
                val = conv_up(d_ff + n * FFN_CHUNK)
                f_buf[r0:r0 + crows, n * FFN_CHUNK:(n + 1) * FFN_CHUNK] = (
                    _gelu_tanh(gate) * val).astype(BF16)

            out = _dot(f_buf[r0:r0 + crows, :], w_down_ref[...])
            res = x + _rmsnorm(out, g_post_ref[...])
            o_ref[:, t0:t0 + STEPS_PER_CHUNK, :] = jnp.swapaxes(
                res.reshape(STEPS_PER_CHUNK, batch, d), 0, 1)

        up_buf[0:tail, :] = up_buf[rows:rows + tail, :]


def _full(a):
    return pl.BlockSpec(a.shape, lambda j: (0,) * a.ndim)


def _weight_rows(w):
    return pl.BlockSpec((w.shape[0] // PACK_STEPS, w.shape[1]),
                        lambda j: (jnp.minimum(j, PACK_STEPS - 1), 0))


def _packed(w):
    return pltpu.VMEM((w.shape[0] // 2, w.shape[1]), jnp.uint32)


def _tile_index(j):
    return jnp.maximum(j - PACK_STEPS, 0)


_COMPILER_PARAMS = pltpu.CompilerParams(
    dimension_semantics=("arbitrary",), vmem_limit_bytes=VMEM_LIMIT_BYTES)


def _mixer(x, g_pre, g_post, w_in, csw, w_cb, lcw, lcb, wa, ba, wx, bx, lam, w_lb, w_out):
    batch, s, d = x.shape
    rows = STEPS_PER_TILE * batch
    matrices = (w_in, w_cb, wa, wx, w_lb, w_out)
    args = (g_pre, g_post, w_in, csw, w_cb, lcw, lcb, wa, ba, wx, bx, lam, w_lb, w_out)
    return pl.pallas_call(
        _mixer_kernel,
        grid=(PACK_STEPS + s // STEPS_PER_TILE,),
        in_specs=[pl.BlockSpec((batch, STEPS_PER_TILE, d), lambda j: (0, _tile_index(j), 0)),
                  _full(g_pre), _full(g_post), _weight_rows(w_in), _full(csw), _weight_rows(w_cb),
                  _full(lcw), _full(lcb), _weight_rows(wa), _full(ba), _weight_rows(wx),
                  _full(bx), _full(lam), _weight_rows(w_lb), _weight_rows(w_out)],
        out_specs=pl.BlockSpec((rows, d), lambda j: (_tile_index(j), 0)),
        out_shape=jax.ShapeDtypeStruct((s * batch, d), x.dtype),
        scratch_shapes=[
            pltpu.VMEM((rows + (csw.shape[0] - 1) * batch, d), F32),
            pltpu.VMEM((rows + (lcw.shape[0] - 1) * batch, d), F32),
            pltpu.VMEM((rows, d), F32),
            pltpu.VMEM((batch, d), F32),
        ] + [_packed(w) for w in matrices],
        compiler_params=_COMPILER_PARAMS,
        name="mixer",
    )(x, *args)


def _convffn(x, batch, g_pre, g_post, w_up, cw, cb, w_down):
    n, d = x.shape
    s = n // batch
    d_ff = w_down.shape[0]
    rows = STEPS_PER_TILE * batch
    return pl.pallas_call(
        _ffn_kernel,
        grid=(PACK_STEPS + s // STEPS_PER_TILE,),
        in_specs=[pl.BlockSpec((rows, d), lambda j: (_tile_index(j), 0)),
                  _full(g_pre), _full(g_post), _weight_rows(w_up), _full(cw), _full(cb),
                  _weight_rows(w_down)],
        out_specs=pl.BlockSpec((batch, STEPS_PER_TILE, d), lambda j: (0, _tile_index(j), 0)),
        out_shape=jax.ShapeDtypeStruct((batch, s, d), x.dtype),
        scratch_shapes=[
            pltpu.VMEM((rows + (cw.shape[0] - 1) * batch, 2 * d_ff), F32),
            pltpu.VMEM((rows, d_ff), BF16),
            _packed(w_up),
            _packed(w_down),
        ],
        compiler_params=_COMPILER_PARAMS,
        name="convffn",
    )(x, g_pre, g_post, w_up, cw, cb, w_down)


def kernel(x, norm_mix_pre, norm_mix_post, norm_ffn_pre, norm_ffn_post, w_in, conv_short_w, w_conv_branch, lru_conv_w, lru_conv_b, lru_wa, lru_ba, lru_wx, lru_bx, lru_lambda, w_lru_branch, w_out, ffn_w_up, ffn_conv_w, ffn_conv_b, ffn_w_down):
    depth = w_in.shape[0]
    bsz, s, d = x.shape
    assert bsz == SUBLANES and s % STEPS_PER_TILE == 0 and STEPS_PER_TILE % STEPS_PER_CHUNK == 0
    assert ffn_w_down.shape[1] % FFN_CHUNK == 0
    for l in range(depth):
        xt = _mixer(
            x, norm_mix_pre[l][None], norm_mix_post[l][None], w_in[l],
            conv_short_w[l], w_conv_branch[l], lru_conv_w[l], lru_conv_b[l][None],
            lru_wa[l].reshape(d, -1), lru_ba[l].reshape(1, d),
            lru_wx[l].reshape(d, -1), lru_bx[l].reshape(1, d),
            lru_lambda[l][None], w_lru_branch[l], w_out[l])
        x = _convffn(
            xt, bsz, norm_ffn_pre[l][None], norm_ffn_post[l][None], ffn_w_up[l],
            ffn_conv_w[l], ffn_conv_b[l][None], ffn_w_down[l])
    return x
```

```python
import math

import jax
import jax.numpy as jnp
from jax import lax
from jax.experimental import pallas as pl
from jax.experimental.pallas import tpu as pltpu

F32 = jnp.float32
BF16 = jnp.bfloat16

LRU_HEADS = 4
LRU_C = 8.0
RMS_EPS = 1e-6
SUBLANES = 8
STEPS_PER_TILE = 64
STEPS_PER_CHUNK = 32
FFN_CHUNK = 512
PACK_STEPS = 16
VMEM_LIMIT_BYTES = 60 * 1024 * 1024


def _rmsnorm(x, g):
    ms = jnp.mean(x * x, axis=-1, keepdims=True)
    return x * lax.rsqrt(ms + RMS_EPS) * g


def _gelu_tanh(x):
    c = math.sqrt(2.0 / math.pi)
    return 0.5 * x * (1.0 + jnp.tanh(c * (x + 0.044715 * (x * x * x))))


def _log_sigmoid(x):
    return jnp.minimum(x, 0.0) - jnp.log1p(jnp.exp(-jnp.abs(x)))


def _pack_rows(step, src_ref, dst_ref):
    kb = src_ref.shape[0] // 2
    start = pl.multiple_of(step * kb, kb)
    dst_ref[pl.ds(start, kb), :] = pltpu.bitcast(src_ref[...].astype(BF16), jnp.uint32)


def _dot(a, w_packed):
    w = pltpu.bitcast(w_packed, BF16)
    return jnp.dot(a, w, preferred_element_type=F32)


def _causal_conv(buf, cur, w_ref, cols, row0, nrows, batch):
    k_width = w_ref.shape[0]
    acc = cur * w_ref[k_width - 1:k_width, cols]
    for k in range(k_width - 1):
        start = row0 + k * batch
        acc = acc + buf[start:start + nrows, cols] * w_ref[k:k + 1, cols]
    return acc


def _mixer_kernel(x_ref, g_pre_ref, g_post_ref, w_in_f32, csw_ref, w_cb_f32, lcw_ref, lcb_ref,
                  wa_f32, ba_ref, wx_f32, bx_ref, lam_ref, w_lb_f32, w_out_f32,
                  o_ref, cv_buf, lx_buf, hs_buf, h_carry,
                  w_in_ref, w_cb_ref, wa_ref, wx_ref, w_lb_ref, w_out_ref):
    batch, steps, d = x_ref.shape
    rows = steps * batch
    crows = STEPS_PER_CHUNK * batch
    j = pl.program_id(0)
    cv_tail = cv_buf.shape[0] - rows
    lx_tail = lx_buf.shape[0] - rows
    all_cols = slice(0, d)
    dh = d // LRU_HEADS

    @pl.when(j == 0)
    def _reset_state():
        cv_buf[0:cv_tail, :] = jnp.zeros((cv_tail, d), F32)
        lx_buf[0:lx_tail, :] = jnp.zeros((lx_tail, d), F32)
        h_carry[...] = jnp.zeros_like(h_carry)

    @pl.when(j < PACK_STEPS)
    def _pack_weights():
        for src, dst in ((w_in_f32, w_in_ref), (w_cb_f32, w_cb_ref), (wa_f32, wa_ref),
                         (wx_f32, wx_ref), (w_lb_f32, w_lb_ref), (w_out_f32, w_out_ref)):
            _pack_rows(j, src, dst)

    @pl.when(j >= PACK_STEPS)
    def _tile():
        log_a_scale = LRU_C * _log_sigmoid(lam_ref[...])
        n_chunks = steps // STEPS_PER_CHUNK
        state = [{} for _ in range(n_chunks)]

        def projections(c):
            s, t0 = state[c], c * STEPS_PER_CHUNK
            x = jnp.swapaxes(x_ref[:, t0:t0 + STEPS_PER_CHUNK, :], 0, 1).reshape(crows, d)
            h = _rmsnorm(x, g_pre_ref[...]).astype(BF16)
            s["x"] = x
            r0 = c * crows

            def proj(k):
                return _dot(h, w_in_ref[:, k * d:(k + 1) * d])

            l_x = proj(3)
            s["c_c"], s["c_x"] = proj(1), proj(2)
            lx_buf[lx_tail + r0:lx_tail + r0 + crows, :] = l_x
            xl = _causal_conv(lx_buf, l_x, lcw_ref, all_cols, r0, crows, batch) + lcb_ref[...]
            xl_b = xl.astype(BF16)
            z_r, z_i = [], []
            for hd, (name, k) in enumerate((("c_b", 0), ("l_y", 4), ("g_conv", 5), ("g_lru", 6))):
                hrows = slice(hd * dh // 2, (hd + 1) * dh // 2)
                z_r.append(_dot(xl_b[:, hd * dh:(hd + 1) * dh], wa_ref[hrows, :]))
                z_i.append(_dot(xl_b[:, hd * dh:(hd + 1) * dh], wx_ref[hrows, :]))
                s[name] = proj(k)
            s["xl"] = xl
            s["z_r"] = jnp.concatenate(z_r, axis=-1)
            s["z_i"] = jnp.concatenate(z_i, axis=-1)

        def gates_and_conv_branch(c):
            s, r0 = state[c], c * crows
            v = s.pop("c_c") * s.pop("c_x")
            cv_buf[cv_tail + r0:cv_tail + r0 + crows, :] = v
            y_a = s.pop("c_b") * _causal_conv(cv_buf, v, csw_ref, all_cols, r0, crows, batch)
            s["br_a"] = _dot(y_a.astype(BF16), w_cb_ref[...])

        def recurrence_branch(c, h_t):
            s, r0 = state[c], c * crows
            r = jax.nn.sigmoid(s.pop("z_r") + ba_ref[...])
            i = jax.nn.sigmoid(s.pop("z_i") + bx_ref[...])
            a = jnp.exp(r * log_a_scale)
            mult = jnp.sqrt(1.0 - a * a)
            if c == 0:
                row = lax.broadcasted_iota(jnp.int32, (crows, d), 0)
                seq_start_rows = jnp.where(j == PACK_STEPS, batch, 0)
                mult = jnp.where(row < seq_start_rows, 1.0, mult)
            u = mult * (i * s.pop("xl"))
            for t in range(STEPS_PER_CHUNK):
                rs = slice(t * batch, (t + 1) * batch)
                h_t = a[rs, :] * h_t + u[rs, :]
                hs_buf[r0 + t * batch:r0 + (t + 1) * batch, :] = h_t
            y_b = hs_buf[r0:r0 + crows, :] * _gelu_tanh(s.pop("l_y"))
            br_b = _dot(y_b.astype(BF16), w_lb_ref[...])
            merged = (jax.nn.sigmoid(s.pop("g_conv")) * s.pop("br_a")
                      + jax.nn.sigmoid(s.pop("g_lru")) * br_b)
            s["merged"] = merged.astype(BF16)
            return h_t

        def output(c):
            s, r0 = state[c], c * crows
            mix = _dot(s.pop("merged"), w_out_ref[...])
            o_ref[r0:r0 + crows, :] = s.pop("x") + _rmsnorm(mix, g_post_ref[...])

        h_t = h_carry[...]
        projections(0)
        for c in range(n_chunks):
            gates_and_conv_branch(c)
            h_t = recurrence_branch(c, h_t)
            if c + 1 < n_chunks:
                projections(c + 1)
            output(c)

        h_carry[...] = h_t
        cv_buf[0:cv_tail, :] = cv_buf[rows:rows + cv_tail, :]
        lx_buf[0:lx_tail, :] = lx_buf[rows:rows + lx_tail, :]


def _ffn_kernel(x_ref, g_pre_ref, g_post_ref, w_up_f32, cw_ref, cb_ref, w_down_f32,
                o_hbm, up_buf, f_buf, res_buf, out_sems, w_up_ref, w_down_ref):
    rows, d = x_ref.shape
    steps, batch, _ = res_buf.shape
    crows = STEPS_PER_CHUNK * batch
    d_ff = f_buf.shape[1]
    j = pl.program_id(0)
    last_step = pl.num_programs(0) - 1
    tail = up_buf.shape[0] - rows

    def output_copy(b, tile):
        return pltpu.make_async_copy(
            res_buf.at[:, b, :], o_hbm.at[b, pl.ds(tile * steps, steps), :], out_sems.at[b])

    @pl.when(j == 0)
    def _reset_state():
        up_buf[0:tail, :] = jnp.zeros((tail, up_buf.shape[1]), F32)
        res_buf[...] = jnp.zeros_like(res_buf)

    @pl.when(j < PACK_STEPS)
    def _pack_weights():
        _pack_rows(j, w_up_f32, w_up_ref)
        _pack_rows(j, w_down_f32, w_down_ref)

    @pl.when(j == PACK_STEPS - 1)
    def _prime_output_copies():
        for b in range(batch):
            output_copy(b, 0).start()

    @pl.when(j >= PACK_STEPS)
    def _tile():
        tile = j - PACK_STEPS
        for b in range(batch):
            output_copy(b, tile).wait()
        for c in range(steps // STEPS_PER_CHUNK):
            r0 = c * crows
            t0 = c * STEPS_PER_CHUNK
            x = x_ref[r0:r0 + crows, :]
            h = _rmsnorm(x, g_pre_ref[...]).astype(BF16)

            def conv_up(col):
                cols = slice(col, col + FFN_CHUNK)
                up = _dot(h, w_up_ref[:, cols])
                up_buf[tail + r0:tail + r0 + crows, cols] = up
                return _causal_conv(up_buf, up, cw_ref, cols, r0, crows, batch) + cb_ref[:, cols]

            for n in range(d_ff // FFN_CHUNK):
                gate = conv_up(n * FFN_CHUNK)
                val = conv_up(d_ff + n * FFN_CHUNK)
                f_buf[r0:r0 + crows, n * FFN_CHUNK:(n + 1) * FFN_CHUNK] = (
                    _gelu_tanh(gate) * val).astype(BF16)

            out = _dot(f_buf[r0:r0 + crows, :], w_down_ref[...])
            res = x + _rmsnorm(out, g_post_ref[...])
            res_buf[t0:t0 + STEPS_PER_CHUNK, :, :] = res.reshape(STEPS_PER_CHUNK, batch, d)

        up_buf[0:tail, :] = up_buf[rows:rows + tail, :]
        for b in range(batch):
            output_copy(b, tile).start()

        @pl.when(j == last_step)
        def _drain_output_copies():
            for b in range(batch):
                output_copy(b, tile).wait()


def _full(a):
    return pl.BlockSpec(a.shape, lambda j: (0,) * a.ndim)


def _weight_rows(w):
    return pl.BlockSpec((w.shape[0] // PACK_STEPS, w.shape[1]),
                        lambda j: (jnp.minimum(j, PACK_STEPS - 1), 0))


def _packed(w):
    return pltpu.VMEM((w.shape[0] // 2, w.shape[1]), jnp.uint32)


def _tile_index(j):
    return jnp.maximum(j - PACK_STEPS, 0)


_COMPILER_PARAMS = pltpu.CompilerParams(
    dimension_semantics=("arbitrary",), vmem_limit_bytes=VMEM_LIMIT_BYTES)


def _mixer(x, g_pre, g_post, w_in, csw, w_cb, lcw, lcb, wa, ba, wx, bx, lam, w_lb, w_out):
    batch, s, d = x.shape
    rows = STEPS_PER_TILE * batch
    args = (g_pre, g_post, w_in, csw, w_cb, lcw, lcb, wa, ba, wx, bx, lam, w_lb, w_out)
    return pl.pallas_call(
        _mixer_kernel,
        grid=(PACK_STEPS + s // STEPS_PER_TILE,),
        in_specs=[pl.BlockSpec((batch, STEPS_PER_TILE, d), lambda j: (0, _tile_index(j), 0)),
                  _full(g_pre), _full(g_post), _weight_rows(w_in), _full(csw), _weight_rows(w_cb),
                  _full(lcw), _full(lcb), _weight_rows(wa), _full(ba), _weight_rows(wx),
                  _full(bx), _full(lam), _weight_rows(w_lb), _weight_rows(w_out)],
        out_specs=pl.BlockSpec((rows, d), lambda j: (_tile_index(j), 0)),
        out_shape=jax.ShapeDtypeStruct((s * batch, d), x.dtype),
        scratch_shapes=[
            pltpu.VMEM((rows + (csw.shape[0] - 1) * batch, d), F32),
            pltpu.VMEM((rows + (lcw.shape[0] - 1) * batch, d), F32),
            pltpu.VMEM((rows, d), F32),
            pltpu.VMEM((batch, d), F32),
        ] + [_packed(w) for w in (w_in, w_cb, wa, wx, w_lb, w_out)],
        compiler_params=_COMPILER_PARAMS,
        name="mixer",
    )(x, *args)


def _convffn(x, batch, g_pre, g_post, w_up, cw, cb, w_down):
    n, d = x.shape
    s = n // batch
    d_ff = w_down.shape[0]
    rows = STEPS_PER_TILE * batch
    return pl.pallas_call(
        _ffn_kernel,
        grid=(PACK_STEPS + s // STEPS_PER_TILE,),
        in_specs=[pl.BlockSpec((rows, d), lambda j: (_tile_index(j), 0)),
                  _full(g_pre), _full(g_post), _weight_rows(w_up), _full(cw), _full(cb),
                  _weight_rows(w_down)],
        out_specs=pl.BlockSpec(memory_space=pl.ANY),
        out_shape=jax.ShapeDtypeStruct((batch, s, d), x.dtype),
        scratch_shapes=[
            pltpu.VMEM((rows + (cw.shape[0] - 1) * batch, 2 * d_ff), F32),
            pltpu.VMEM((rows, d_ff), BF16),
            pltpu.VMEM((STEPS_PER_TILE, batch, d), F32),
            pltpu.SemaphoreType.DMA((batch,)),
            _packed(w_up),
            _packed(w_down),
        ],
        compiler_params=_COMPILER_PARAMS,
        name="convffn",
    )(x, g_pre, g_post, w_up, cw, cb, w_down)


def kernel(x, norm_mix_pre, norm_mix_post, norm_ffn_pre, norm_ffn_post, w_in, conv_short_w, w_conv_branch, lru_conv_w, lru_conv_b, lru_wa, lru_ba, lru_wx, lru_bx, lru_lambda, w_lru_branch, w_out, ffn_w_up, ffn_conv_w, ffn_conv_b, ffn_w_down):
    depth = w_in.shape[0]
    bsz, s, d = x.shape
    assert bsz == SUBLANES and s % STEPS_PER_TILE == 0 and STEPS_PER_TILE % STEPS_PER_CHUNK == 0
    assert ffn_w_down.shape[1] % FFN_CHUNK == 0
    for l in range(depth):
        xt = _mixer(
            x, norm_mix_pre[l][None], norm_mix_post[l][None], w_in[l],
            conv_short_w[l], w_conv_branch[l], lru_conv_w[l], lru_conv_b[l][None],
            lru_wa[l].reshape(d, -1), lru_ba[l].reshape(1, d),
            lru_wx[l].reshape(d, -1), lru_bx[l].reshape(1, d),
            lru_lambda[l][None], w_lru_branch[l], w_out[l])
        x = _convffn(
            xt, bsz, norm_ffn_pre[l][None], norm_ffn_post[l][None], ffn_w_up[l],
            ffn_conv_w[l], ffn_conv_b[l][None], ffn_w_down[l])
    return x
```

```python
import math

import jax
import jax.numpy as jnp
from jax import lax
from jax.experimental import pallas as pl
from jax.experimental.pallas import tpu as pltpu

F32 = jnp.float32
BF16 = jnp.bfloat16

LRU_HEADS = 4
LRU_C = 8.0
RMS_EPS = 1e-6
SUBLANES = 8
STEPS_PER_TILE = 64
STEPS_PER_CHUNK = 32
FFN_CHUNK = 512
PACK_STEPS = 8
VMEM_LIMIT_BYTES = 60 * 1024 * 1024


def _rmsnorm(x, g):
    ms = jnp.mean(x * x, axis=-1, keepdims=True)
    return x * lax.rsqrt(ms + RMS_EPS) * g


def _gelu_tanh(x):
    c = math.sqrt(2.0 / math.pi)
    return 0.5 * x * (1.0 + jnp.tanh(c * (x + 0.044715 * (x * x * x))))


def _log_sigmoid(x):
    return jnp.minimum(x, 0.0) - jnp.log1p(jnp.exp(-jnp.abs(x)))


def _pack_rows(step, src_ref, dst_ref):
    kb = src_ref.shape[0] // 2
    start = pl.multiple_of(step * kb, kb)
    dst_ref[pl.ds(start, kb), :] = pltpu.bitcast(src_ref[...].astype(BF16), jnp.uint32)


def _dot(a, w_packed):
    w = pltpu.bitcast(w_packed, BF16)
    return jnp.dot(a, w, preferred_element_type=F32)


def _causal_conv(buf, cur, w_ref, cols, row0, nrows, batch):
    k_width = w_ref.shape[0]
    acc = cur * w_ref[k_width - 1:k_width, cols]
    for k in range(k_width - 1):
        start = row0 + k * batch
        acc = acc + buf[start:start + nrows, cols] * w_ref[k:k + 1, cols]
    return acc


def _mixer_kernel(x_ref, g_pre_ref, g_post_ref, w_in_f32, csw_ref, w_cb_f32, lcw_ref, lcb_ref,
                  wa_f32, ba_ref, wx_f32, bx_ref, lam_ref, w_lb_f32, w_out_f32,
                  o_ref, cv_buf, lx_buf, hs_buf, h_carry,
                  w_in_ref, w_cb_ref, wa_ref, wx_ref, w_lb_ref, w_out_ref):
    batch, steps, d = x_ref.shape
    rows = steps * batch
    crows = STEPS_PER_CHUNK * batch
    j = pl.program_id(0)
    cv_tail = cv_buf.shape[0] - rows
    lx_tail = lx_buf.shape[0] - rows
    all_cols = slice(0, d)
    dh = d // LRU_HEADS

    @pl.when(j == 0)
    def _reset_state():
        cv_buf[0:cv_tail, :] = jnp.zeros((cv_tail, d), F32)
        lx_buf[0:lx_tail, :] = jnp.zeros((lx_tail, d), F32)
        h_carry[...] = jnp.zeros_like(h_carry)

    @pl.when(j < PACK_STEPS)
    def _pack_weights():
        for src, dst in ((w_in_f32, w_in_ref), (w_cb_f32, w_cb_ref), (wa_f32, wa_ref),
                         (wx_f32, wx_ref), (w_lb_f32, w_lb_ref), (w_out_f32, w_out_ref)):
            _pack_rows(j, src, dst)

    @pl.when(j >= PACK_STEPS)
    def _tile():
        log_a_scale = LRU_C * _log_sigmoid(lam_ref[...])
        n_chunks = steps // STEPS_PER_CHUNK
        state = [{} for _ in range(n_chunks)]

        def projections(c):
            s, t0 = state[c], c * STEPS_PER_CHUNK
            x = jnp.swapaxes(x_ref[:, t0:t0 + STEPS_PER_CHUNK, :], 0, 1).reshape(crows, d)
            h = _rmsnorm(x, g_pre_ref[...]).astype(BF16)
            s["x"] = x
            r0 = c * crows

            def proj(k):
                return _dot(h, w_in_ref[:, k * d:(k + 1) * d])

            l_x = proj(3)
            s["c_c"], s["c_x"] = proj(1), proj(2)
            lx_buf[lx_tail + r0:lx_tail + r0 + crows, :] = l_x
            xl = _causal_conv(lx_buf, l_x, lcw_ref, all_cols, r0, crows, batch) + lcb_ref[...]
            xl_b = xl.astype(BF16)
            z_r, z_i = [], []
            for hd, (name, k) in enumerate((("c_b", 0), ("l_y", 4), ("g_conv", 5), ("g_lru", 6))):
                hrows = slice(hd * dh // 2, (hd + 1) * dh // 2)
                z_r.append(_dot(xl_b[:, hd * dh:(hd + 1) * dh], wa_ref[hrows, :]))
                z_i.append(_dot(xl_b[:, hd * dh:(hd + 1) * dh], wx_ref[hrows, :]))
                s[name] = proj(k)
            s["xl"] = xl
            s["z_r"] = jnp.concatenate(z_r, axis=-1)
            s["z_i"] = jnp.concatenate(z_i, axis=-1)

        def gates_and_conv_branch(c):
            s, r0 = state[c], c * crows
            v = s.pop("c_c") * s.pop("c_x")
            cv_buf[cv_tail + r0:cv_tail + r0 + crows, :] = v
            y_a = s.pop("c_b") * _causal_conv(cv_buf, v, csw_ref, all_cols, r0, crows, batch)
            s["br_a"] = _dot(y_a.astype(BF16), w_cb_ref[...])

        def recurrence_branch(c, h_t):
            s, r0 = state[c], c * crows
            r = jax.nn.sigmoid(s.pop("z_r") + ba_ref[...])
            i = jax.nn.sigmoid(s.pop("z_i") + bx_ref[...])
            a = jnp.exp(r * log_a_scale)
            mult = jnp.sqrt(1.0 - a * a)
            if c == 0:
                row = lax.broadcasted_iota(jnp.int32, (crows, d), 0)
                seq_start_rows = jnp.where(j == PACK_STEPS, batch, 0)
                mult = jnp.where(row < seq_start_rows, 1.0, mult)
            u = mult * (i * s.pop("xl"))
            for t in range(STEPS_PER_CHUNK):
                rs = slice(t * batch, (t + 1) * batch)
                h_t = a[rs, :] * h_t + u[rs, :]
                hs_buf[r0 + t * batch:r0 + (t + 1) * batch, :] = h_t
            y_b = hs_buf[r0:r0 + crows, :] * _gelu_tanh(s.pop("l_y"))
            br_b = _dot(y_b.astype(BF16), w_lb_ref[...])
            merged = (jax.nn.sigmoid(s.pop("g_conv")) * s.pop("br_a")
                      + jax.nn.sigmoid(s.pop("g_lru")) * br_b)
            s["merged"] = merged.astype(BF16)
            return h_t

        def output(c):
            s, r0 = state[c], c * crows
            mix = _dot(s.pop("merged"), w_out_ref[...])
            o_ref[r0:r0 + crows, :] = s.pop("x") + _rmsnorm(mix, g_post_ref[...])

        h_t = h_carry[...]
        projections(0)
        for c in range(n_chunks):
            gates_and_conv_branch(c)
            h_t = recurrence_branch(c, h_t)
            if c + 1 < n_chunks:
                projections(c + 1)
            output(c)

        h_carry[...] = h_t
        cv_buf[0:cv_tail, :] = cv_buf[rows:rows + cv_tail, :]
        lx_buf[0:lx_tail, :] = lx_buf[rows:rows + lx_tail, :]


def _ffn_kernel(x_ref, g_pre_ref, g_post_ref, w_up_f32, cw_ref, cb_ref, w_down_f32,
                o_ref, up_buf, f_buf, w_up_ref, w_down_ref):
    rows, d = x_ref.shape
    batch, steps, _ = o_ref.shape
    crows = STEPS_PER_CHUNK * batch
    d_ff = f_buf.shape[1]
    j = pl.program_id(0)
    tail = up_buf.shape[0] - rows

    @pl.when(j == 0)
    def _reset_state():
        up_buf[0:tail, :] = jnp.zeros((tail, up_buf.shape[1]), F32)

    @pl.when(j < PACK_STEPS)
    def _pack_weights():
        _pack_rows(j, w_up_f32, w_up_ref)
        _pack_rows(j, w_down_f32, w_down_ref)

    @pl.when(j >= PACK_STEPS)
    def _tile():
        for c in range(steps // STEPS_PER_CHUNK):
            r0 = c * crows
            t0 = c * STEPS_PER_CHUNK
            x = x_ref[r0:r0 + crows, :]
            h = _rmsnorm(x, g_pre_ref[...]).astype(BF16)

            def conv_up(col):
                cols = slice(col, col + FFN_CHUNK)
                up = _dot(h, w_up_ref[:, cols])
                up_buf[tail + r0:tail + r0 + crows, cols] = up
                return _causal_conv(up_buf, up, cw_ref, cols, r0, crows, batch) + cb_ref[:, cols]

            for n in range(d_ff // FFN_CHUNK):
                gate = conv_up(n * FFN_CHUNK)
                val = conv_up(d_ff + n * FFN_CHUNK)
                f_buf[r0:r0 + crows, n * FFN_CHUNK:(n + 1) * FFN_CHUNK] = (
                    _gelu_tanh(gate) * val).astype(BF16)

            out = _dot(f_buf[r0:r0 + crows, :], w_down_ref[...])
            res = x + _rmsnorm(out, g_post_ref[...])
            o_ref[:, t0:t0 + STEPS_PER_CHUNK, :] = jnp.swapaxes(
                res.reshape(STEPS_PER_CHUNK, batch, d), 0, 1)

        up_buf[0:tail, :] = up_buf[rows:rows + tail, :]


def _full(a):
    return pl.BlockSpec(a.shape, lambda j: (0,) * a.ndim)


def _weight_rows(w):
    return pl.BlockSpec((w.shape[0] // PACK_STEPS, w.shape[1]),
                        lambda j: (jnp.minimum(j, PACK_STEPS - 1), 0))


def _packed(w):
    return pltpu.VMEM((w.shape[0] // 2, w.shape[1]), jnp.uint32)


def _tile_index(j):
    return jnp.maximum(j - PACK_STEPS, 0)


_COMPILER_PARAMS = pltpu.CompilerParams(
    dimension_semantics=("arbitrary",), vmem_limit_bytes=VMEM_LIMIT_BYTES)


def _mixer(x, g_pre, g_post, w_in, csw, w_cb, lcw, lcb, wa, ba, wx, bx, lam, w_lb, w_out):
    batch, s, d = x.shape
    rows = STEPS_PER_TILE * batch
    args = (g_pre, g_post, w_in, csw, w_cb, lcw, lcb, wa, ba, wx, bx, lam, w_lb, w_out)
    return pl.pallas_call(
        _mixer_kernel,
        grid=(PACK_STEPS + s // STEPS_PER_TILE,),
        in_specs=[pl.BlockSpec((batch, STEPS_PER_TILE, d), lambda j: (0, _tile_index(j), 0)),
                  _full(g_pre), _full(g_post), _weight_rows(w_in), _full(csw), _weight_rows(w_cb),
                  _full(lcw), _full(lcb), _weight_rows(wa), _full(ba), _weight_rows(wx),
                  _full(bx), _full(lam), _weight_rows(w_lb), _weight_rows(w_out)],
        out_specs=pl.BlockSpec((rows, d), lambda j: (_tile_index(j), 0)),
        out_shape=jax.ShapeDtypeStruct((s * batch, d), x.dtype),
        scratch_shapes=[
            pltpu.VMEM((rows + (csw.shape[0] - 1) * batch, d), F32),
            pltpu.VMEM((rows + (lcw.shape[0] - 1) * batch, d), F32),
            pltpu.VMEM((rows, d), F32),
            pltpu.VMEM((batch, d), F32),
        ] + [_packed(w) for w in (w_in, w_cb, wa, wx, w_lb, w_out)],
        compiler_params=_COMPILER_PARAMS,
        name="mixer",
    )(x, *args)


def _convffn(x, batch, g_pre, g_post, w_up, cw, cb, w_down):
    n, d = x.shape
    s = n // batch
    d_ff = w_down.shape[0]
    rows = STEPS_PER_TILE * batch
    return pl.pallas_call(
        _ffn_kernel,
        grid=(PACK_STEPS + s // STEPS_PER_TILE,),
        in_specs=[pl.BlockSpec((rows, d), lambda j: (_tile_index(j), 0)),
                  _full(g_pre), _full(g_post), _weight_rows(w_up), _full(cw), _full(cb),
                  _weight_rows(w_down)],
        out_specs=pl.BlockSpec((batch, STEPS_PER_TILE, d), lambda j: (0, _tile_index(j), 0)),
        out_shape=jax.ShapeDtypeStruct((batch, s, d), x.dtype),
        scratch_shapes=[
            pltpu.VMEM((rows + (cw.shape[0] - 1) * batch, 2 * d_ff), F32),
            pltpu.VMEM((rows, d_ff), BF16),
            _packed(w_up),
            _packed(w_down),
        ],
        compiler_params=_COMPILER_PARAMS,
        name="convffn",
    )(x, g_pre, g_post, w_up, cw, cb, w_down)


def kernel(x, norm_mix_pre, norm_mix_post, norm_ffn_pre, norm_ffn_post, w_in, conv_short_w, w_conv_branch, lru_conv_w, lru_conv_b, lru_wa, lru_ba, lru_wx, lru_bx, lru_lambda, w_lru_branch, w_out, ffn_w_up, ffn_conv_w, ffn_conv_b, ffn_w_down):
    depth = w_in.shape[0]
    bsz, s, d = x.shape
    assert bsz == SUBLANES and s % STEPS_PER_TILE == 0 and STEPS_PER_TILE % STEPS_PER_CHUNK == 0
    assert ffn_w_down.shape[1] % FFN_CHUNK == 0
    for l in range(depth):
        xt = _mixer(
            x, norm_mix_pre[l][None], norm_mix_post[l][None], w_in[l],
            conv_short_w[l], w_conv_branch[l], lru_conv_w[l], lru_conv_b[l][None],
            lru_wa[l].reshape(d, -1), lru_ba[l].reshape(1, d),
            lru_wx[l].reshape(d, -1), lru_bx[l].reshape(1, d),
            lru_lambda[l][None], w_lru_branch[l], w_out[l])
        x = _convffn(
            xt, bsz, norm_ffn_pre[l][None], norm_ffn_post[l][None], ffn_w_up[l],
            ffn_conv_w[l], ffn_conv_b[l][None], ffn_w_down[l])
    return x
```

```python
import math

import jax
import jax.numpy as jnp
from jax import lax
from jax.experimental import pallas as pl
from jax.experimental.pallas import tpu as pltpu

F32 = jnp.float32
BF16 = jnp.bfloat16

LRU_HEADS = 4
LRU_C = 8.0
RMS_EPS = 1e-6
SUBLANES = 8
STEPS_PER_TILE = 64
STEPS_PER_CHUNK = 32
FFN_CHUNK = 512
PACK_STEPS = 8
VMEM_LIMIT_BYTES = 60 * 1024 * 1024


def _rmsnorm(x, g):
    ms = jnp.mean(x * x, axis=-1, keepdims=True)
    return x * lax.rsqrt(ms + RMS_EPS) * g


def _gelu_tanh(x):
    c = math.sqrt(2.0 / math.pi)
    return 0.5 * x * (1.0 + jnp.tanh(c * (x + 0.044715 * (x * x * x))))


def _log_sigmoid(x):
    return jnp.minimum(x, 0.0) - jnp.log1p(jnp.exp(-jnp.abs(x)))


def _pack_rows(step, src_ref, dst_ref):
    kb = src_ref.shape[0] // 2
    start = pl.multiple_of(step * kb, kb)
    dst_ref[pl.ds(start, kb), :] = pltpu.bitcast(src_ref[...].astype(BF16), jnp.uint32)


def _dot(a, w_packed):
    w = pltpu.bitcast(w_packed, BF16)
    return jnp.dot(a, w, preferred_element_type=F32)


def _causal_conv(buf, cur, w_ref, cols, row0, nrows, batch):
    k_width = w_ref.shape[0]
    acc = cur * w_ref[k_width - 1:k_width, cols]
    for k in range(k_width - 1):
        start = row0 + k * batch
        acc = acc + buf[start:start + nrows, cols] * w_ref[k:k + 1, cols]
    return acc


def _mixer_kernel(x_ref, g_pre_ref, g_post_ref, w_in_f32, csw_ref, w_cb_f32, lcw_ref, lcb_ref,
                  wa_f32, ba_ref, wx_f32, bx_ref, lam_ref, w_lb_f32, w_out_f32,
                  o_ref, cv_buf, lx_buf, hs_buf, h_carry,
                  w_in_ref, w_cb_ref, wa_ref, wx_ref, w_lb_ref, w_out_ref):
    batch, steps, d = x_ref.shape
    rows = steps * batch
    crows = STEPS_PER_CHUNK * batch
    j = pl.program_id(0)
    cv_tail = cv_buf.shape[0] - rows
    lx_tail = lx_buf.shape[0] - rows
    all_cols = slice(0, d)
    dh = d // LRU_HEADS

    @pl.when(j == 0)
    def _reset_state():
        cv_buf[0:cv_tail, :] = jnp.zeros((cv_tail, d), F32)
        lx_buf[0:lx_tail, :] = jnp.zeros((lx_tail, d), F32)
        h_carry[...] = jnp.zeros_like(h_carry)

    @pl.when(j < PACK_STEPS)
    def _pack_weights():
        for src, dst in ((w_in_f32, w_in_ref), (w_cb_f32, w_cb_ref), (wa_f32, wa_ref),
                         (wx_f32, wx_ref), (w_lb_f32, w_lb_ref), (w_out_f32, w_out_ref)):
            _pack_rows(j, src, dst)

    @pl.when(j >= PACK_STEPS)
    def _tile():
        log_a_scale = LRU_C * _log_sigmoid(lam_ref[...])
        n_chunks = steps // STEPS_PER_CHUNK
        state = [{} for _ in range(n_chunks)]

        lru_state = {"h_t": h_carry[...]}

        def chunk_steps(c):
            s, r0, t0 = state[c], c * crows, c * STEPS_PER_CHUNK

            def prenorm():
                x = jnp.swapaxes(x_ref[:, t0:t0 + STEPS_PER_CHUNK, :], 0, 1).reshape(crows, d)
                s["x"] = x
                s["h"] = _rmsnorm(x, g_pre_ref[...]).astype(BF16)

            def proj(name, k):
                def run():
                    s[name] = _dot(s["h"], w_in_ref[:, k * d:(k + 1) * d])
                return run

            def lru_conv():
                l_x = s.pop("l_x")
                lx_buf[lx_tail + r0:lx_tail + r0 + crows, :] = l_x
                xl = _causal_conv(lx_buf, l_x, lcw_ref, all_cols, r0, crows, batch) + lcb_ref[...]
                s["xl"], s["xl_b"] = xl, xl.astype(BF16)

            def gate(name, w_ref, hd):
                def run():
                    s[name, hd] = _dot(s["xl_b"][:, hd * dh:(hd + 1) * dh],
                                       w_ref[hd * dh // 2:(hd + 1) * dh // 2, :])
                return run

            def conv_branch():
                v = s.pop("c_c") * s.pop("c_x")
                cv_buf[cv_tail + r0:cv_tail + r0 + crows, :] = v
                y_a = s.pop("c_b") * _causal_conv(cv_buf, v, csw_ref, all_cols, r0, crows, batch)
                s["br_a"] = _dot(y_a.astype(BF16), w_cb_ref[...])

            def recurrence():
                z_r = jnp.concatenate([s.pop(("z_r", hd)) for hd in range(LRU_HEADS)], axis=-1)
                z_i = jnp.concatenate([s.pop(("z_i", hd)) for hd in range(LRU_HEADS)], axis=-1)
                r = jax.nn.sigmoid(z_r + ba_ref[...])
                i = jax.nn.sigmoid(z_i + bx_ref[...])
                a = jnp.exp(r * log_a_scale)
                mult = jnp.sqrt(1.0 - a * a)
                if c == 0:
                    row = lax.broadcasted_iota(jnp.int32, (crows, d), 0)
                    seq_start_rows = jnp.where(j == PACK_STEPS, batch, 0)
                    mult = jnp.where(row < seq_start_rows, 1.0, mult)
                u = mult * (i * s.pop("xl"))
                h_t = lru_state["h_t"]
                for t in range(STEPS_PER_CHUNK):
                    rs = slice(t * batch, (t + 1) * batch)
                    h_t = a[rs, :] * h_t + u[rs, :]
                    hs_buf[r0 + t * batch:r0 + (t + 1) * batch, :] = h_t
                lru_state["h_t"] = h_t
                s["y_b"] = (hs_buf[r0:r0 + crows, :] * _gelu_tanh(s.pop("l_y"))).astype(BF16)

            def lru_branch_and_merge():
                br_b = _dot(s.pop("y_b"), w_lb_ref[...])
                merged = (jax.nn.sigmoid(s.pop("g_conv")) * s.pop("br_a")
                          + jax.nn.sigmoid(s.pop("g_lru")) * br_b)
                s["merged"] = merged.astype(BF16)

            def output():
                mix = _dot(s.pop("merged"), w_out_ref[...])
                o_ref[r0:r0 + crows, :] = s.pop("x") + _rmsnorm(mix, g_post_ref[...])

            head = [prenorm, proj("l_x", 3), proj("c_c", 1)]
            body = [lru_conv, gate("z_r", wa_ref, 0), proj("c_x", 2), gate("z_i", wx_ref, 0),
                    proj("c_b", 0), gate("z_r", wa_ref, 1), proj("l_y", 4), gate("z_i", wx_ref, 1),
                    proj("g_conv", 5), gate("z_r", wa_ref, 2), proj("g_lru", 6),
                    gate("z_i", wx_ref, 2), conv_branch, gate("z_r", wa_ref, 3),
                    gate("z_i", wx_ref, 3), recurrence]
            return head, body, lru_branch_and_merge, output

        plans = [chunk_steps(c) for c in range(n_chunks)]
        for step in plans[0][0] + plans[0][1]:
            step()
        for c in range(n_chunks):
            _, _, lru_branch_and_merge, output = plans[c]
            if c + 1 < n_chunks:
                next_head, next_body = plans[c + 1][0], plans[c + 1][1]
                for step in next_head:
                    step()
                lru_branch_and_merge()
                for step in next_body[:3]:
                    step()
                output()
                for step in next_body[3:]:
                    step()
            else:
                lru_branch_and_merge()
                output()
        h_t = lru_state["h_t"]

        h_carry[...] = h_t
        cv_buf[0:cv_tail, :] = cv_buf[rows:rows + cv_tail, :]
        lx_buf[0:lx_tail, :] = lx_buf[rows:rows + lx_tail, :]


def _ffn_kernel(x_ref, g_pre_ref, g_post_ref, w_up_f32, cw_ref, cb_ref, w_down_f32,
                o_ref, up_buf, f_buf, w_up_ref, w_down_ref):
    rows, d = x_ref.shape
    batch, steps, _ = o_ref.shape
    crows = STEPS_PER_CHUNK * batch
    d_ff = f_buf.shape[1]
    j = pl.program_id(0)
    tail = up_buf.shape[0] - rows

    @pl.when(j == 0)
    def _reset_state():
        up_buf[0:tail, :] = jnp.zeros((tail, up_buf.shape[1]), F32)

    @pl.when(j < PACK_STEPS)
    def _pack_weights():
        _pack_rows(j, w_up_f32, w_up_ref)
        _pack_rows(j, w_down_f32, w_down_ref)

    @pl.when(j >= PACK_STEPS)
    def _tile():
        for c in range(steps // STEPS_PER_CHUNK):
            r0 = c * crows
            t0 = c * STEPS_PER_CHUNK
            x = x_ref[r0:r0 + crows, :]
            h = _rmsnorm(x, g_pre_ref[...]).astype(BF16)

            def conv_up(col):
                cols = slice(col, col + FFN_CHUNK)
                up = _dot(h, w_up_ref[:, cols])
                up_buf[tail + r0:tail + r0 + crows, cols] = up
                return _causal_conv(up_buf, up, cw_ref, cols, r0, crows, batch) + cb_ref[:, cols]

            for n in range(d_ff // FFN_CHUNK):
                gate = conv_up(n * FFN_CHUNK)
                val = conv_up(d_ff + n * FFN_CHUNK)
                f_buf[r0:r0 + crows, n * FFN_CHUNK:(n + 1) * FFN_CHUNK] = (
                    _gelu_tanh(gate) * val).astype(BF16)

            out = _dot(f_buf[r0:r0 + crows, :], w_down_ref[...])
            res = x + _rmsnorm(out, g_post_ref[...])
            o_ref[:, t0:t0 + STEPS_PER_CHUNK, :] = jnp.swapaxes(
                res.reshape(STEPS_PER_CHUNK, batch, d), 0, 1)

        up_buf[0:tail, :] = up_buf[rows:rows + tail, :]


def _full(a):
    return pl.BlockSpec(a.shape, lambda j: (0,) * a.ndim)


def _weight_rows(w):
    return pl.BlockSpec((w.shape[0] // PACK_STEPS, w.shape[1]),
                        lambda j: (jnp.minimum(j, PACK_STEPS - 1), 0))


def _packed(w):
    return pltpu.VMEM((w.shape[0] // 2, w.shape[1]), jnp.uint32)


def _tile_index(j):
    return jnp.maximum(j - PACK_STEPS, 0)


_COMPILER_PARAMS = pltpu.CompilerParams(
    dimension_semantics=("arbitrary",), vmem_limit_bytes=VMEM_LIMIT_BYTES)


def _mixer(x, g_pre, g_post, w_in, csw, w_cb, lcw, lcb, wa, ba, wx, bx, lam, w_lb, w_out):
    batch, s, d = x.shape
    rows = STEPS_PER_TILE * batch
    args = (g_pre, g_post, w_in, csw, w_cb, lcw, lcb, wa, ba, wx, bx, lam, w_lb, w_out)
    return pl.pallas_call(
        _mixer_kernel,
        grid=(PACK_STEPS + s // STEPS_PER_TILE,),
        in_specs=[pl.BlockSpec((batch, STEPS_PER_TILE, d), lambda j: (0, _tile_index(j), 0)),
                  _full(g_pre), _full(g_post), _weight_rows(w_in), _full(csw), _weight_rows(w_cb),
                  _full(lcw), _full(lcb), _weight_rows(wa), _full(ba), _weight_rows(wx),
                  _full(bx), _full(lam), _weight_rows(w_lb), _weight_rows(w_out)],
        out_specs=pl.BlockSpec((rows, d), lambda j: (_tile_index(j), 0)),
        out_shape=jax.ShapeDtypeStruct((s * batch, d), x.dtype),
        scratch_shapes=[
            pltpu.VMEM((rows + (csw.shape[0] - 1) * batch, d), F32),
            pltpu.VMEM((rows + (lcw.shape[0] - 1) * batch, d), F32),
            pltpu.VMEM((rows, d), F32),
            pltpu.VMEM((batch, d), F32),
        ] + [_packed(w) for w in (w_in, w_cb, wa, wx, w_lb, w_out)],
        compiler_params=_COMPILER_PARAMS,
        name="mixer",
    )(x, *args)


def _convffn(x, batch, g_pre, g_post, w_up, cw, cb, w_down):
    n, d = x.shape
    s = n // batch
    d_ff = w_down.shape[0]
    rows = STEPS_PER_TILE * batch
    return pl.pallas_call(
        _ffn_kernel,
        grid=(PACK_STEPS + s // STEPS_PER_TILE,),
        in_specs=[pl.BlockSpec((rows, d), lambda j: (_tile_index(j), 0)),
                  _full(g_pre), _full(g_post), _weight_rows(w_up), _full(cw), _full(cb),
                  _weight_rows(w_down)],
        out_specs=pl.BlockSpec((batch, STEPS_PER_TILE, d), lambda j: (0, _tile_index(j), 0)),
        out_shape=jax.ShapeDtypeStruct((batch, s, d), x.dtype),
        scratch_shapes=[
            pltpu.VMEM((rows + (cw.shape[0] - 1) * batch, 2 * d_ff), F32),
            pltpu.VMEM((rows, d_ff), BF16),
            _packed(w_up),
            _packed(w_down),
        ],
        compiler_params=_COMPILER_PARAMS,
        name="convffn",
    )(x, g_pre, g_post, w_up, cw, cb, w_down)


def kernel(x, norm_mix_pre, norm_mix_post, norm_ffn_pre, norm_ffn_post, w_in, conv_short_w, w_conv_branch, lru_conv_w, lru_conv_b, lru_wa, lru_ba, lru_wx, lru_bx, lru_lambda, w_lru_branch, w_out, ffn_w_up, ffn_conv_w, ffn_conv_b, ffn_w_down):
    depth = w_in.shape[0]
    bsz, s, d = x.shape
    assert bsz == SUBLANES and s % STEPS_PER_TILE == 0 and STEPS_PER_TILE % STEPS_PER_CHUNK == 0
    assert ffn_w_down.shape[1] % FFN_CHUNK == 0
    for l in range(depth):
        xt = _mixer(
            x, norm_mix_pre[l][None], norm_mix_post[l][None], w_in[l],
            conv_short_w[l], w_conv_branch[l], lru_conv_w[l], lru_conv_b[l][None],
            lru_wa[l].reshape(d, -1), lru_ba[l].reshape(1, d),
            lru_wx[l].reshape(d, -1), lru_bx[l].reshape(1, d),
            lru_lambda[l][None], w_lru_branch[l], w_out[l])
        x = _convffn(
            xt, bsz, norm_ffn_pre[l][None], norm_ffn_post[l][None], ffn_w_up[l],
            ffn_conv_w[l], ffn_conv_b[l][None], ffn_w_down[l])
    return x
```

```python
import math

import jax
import jax.numpy as jnp
from jax import lax
from jax.experimental import pallas as pl
from jax.experimental.pallas import tpu as pltpu

F32 = jnp.float32
BF16 = jnp.bfloat16

LRU_HEADS = 4
LRU_C = 8.0
RMS_EPS = 1e-6
SUBLANES = 8
STEPS_PER_TILE = 64
STEPS_PER_CHUNK = 32
FFN_CHUNK = 512
PACK_STEPS = 8
VMEM_LIMIT_BYTES = 60 * 1024 * 1024


def _rmsnorm(x, g):
    ms = jnp.mean(x * x, axis=-1, keepdims=True)
    return x * lax.rsqrt(ms + RMS_EPS) * g


def _gelu_tanh(x):
    c = math.sqrt(2.0 / math.pi)
    return 0.5 * x * (1.0 + jnp.tanh(c * (x + 0.044715 * (x * x * x))))


def _log_sigmoid(x):
    return jnp.minimum(x, 0.0) - jnp.log1p(jnp.exp(-jnp.abs(x)))


def _pack_rows(step, src_ref, dst_ref):
    kb = src_ref.shape[0] // 2
    start = pl.multiple_of(step * kb, kb)
    dst_ref[pl.ds(start, kb), :] = pltpu.bitcast(src_ref[...].astype(BF16), jnp.uint32)


def _dot(a, w_packed):
    w = pltpu.bitcast(w_packed, BF16)
    return jnp.dot(a, w, preferred_element_type=F32)


def _causal_conv(buf, cur, w_ref, cols, row0, nrows, batch):
    k_width = w_ref.shape[0]
    acc = cur * w_ref[k_width - 1:k_width, cols]
    for k in range(k_width - 1):
        start = row0 + k * batch
        acc = acc + buf[start:start + nrows, cols] * w_ref[k:k + 1, cols]
    return acc


def _mixer_kernel(x_ref, g_pre_ref, g_post_ref, w_in_f32, csw_ref, w_cb_f32, lcw_ref, lcb_ref,
                  wa_f32, ba_ref, wx_f32, bx_ref, lam_ref, w_lb_f32, w_out_f32,
                  o_ref, cv_buf, lx_buf, hs_buf, h_carry,
                  w_in_ref, w_cb_ref, wa_ref, wx_ref, w_lb_ref, w_out_ref):
    batch, steps, d = x_ref.shape
    rows = steps * batch
    crows = STEPS_PER_CHUNK * batch
    j = pl.program_id(0)
    cv_tail = cv_buf.shape[0] - rows
    lx_tail = lx_buf.shape[0] - rows
    all_cols = slice(0, d)
    dh = d // LRU_HEADS

    @pl.when(j == 0)
    def _reset_state():
        cv_buf[0:cv_tail, :] = jnp.zeros((cv_tail, d), F32)
        lx_buf[0:lx_tail, :] = jnp.zeros((lx_tail, d), F32)
        h_carry[...] = jnp.zeros_like(h_carry)

    @pl.when(j < PACK_STEPS)
    def _pack_weights():
        for src, dst in ((w_in_f32, w_in_ref), (w_cb_f32, w_cb_ref), (wa_f32, wa_ref),
                         (wx_f32, wx_ref), (w_lb_f32, w_lb_ref), (w_out_f32, w_out_ref)):
            _pack_rows(j, src, dst)

    @pl.when(j >= PACK_STEPS)
    def _tile():
        log_a_scale = LRU_C * _log_sigmoid(lam_ref[...])
        n_chunks = steps // STEPS_PER_CHUNK
        state = [{} for _ in range(n_chunks)]

        lru_state = {"h_t": h_carry[...]}

        def chunk_steps(c):
            s, r0, t0 = state[c], c * crows, c * STEPS_PER_CHUNK

            def prenorm():
                x = jnp.swapaxes(x_ref[:, t0:t0 + STEPS_PER_CHUNK, :], 0, 1).reshape(crows, d)
                s["x"] = x
                s["h"] = _rmsnorm(x, g_pre_ref[...]).astype(BF16)

            def proj(name, k):
                def run():
                    s[name] = _dot(s["h"], w_in_ref[:, k * d:(k + 1) * d])
                return run

            def lru_conv():
                l_x = s.pop("l_x")
                lx_buf[lx_tail + r0:lx_tail + r0 + crows, :] = l_x
                xl = _causal_conv(lx_buf, l_x, lcw_ref, all_cols, r0, crows, batch) + lcb_ref[...]
                s["xl"], s["xl_b"] = xl, xl.astype(BF16)

            def gate(name, w_ref, hd):
                def run():
                    s[name, hd] = _dot(s["xl_b"][:, hd * dh:(hd + 1) * dh],
                                       w_ref[hd * dh // 2:(hd + 1) * dh // 2, :])
                return run

            def conv_branch():
                v = s.pop("c_c") * s.pop("c_x")
                cv_buf[cv_tail + r0:cv_tail + r0 + crows, :] = v
                y_a = s.pop("c_b") * _causal_conv(cv_buf, v, csw_ref, all_cols, r0, crows, batch)
                s["br_a"] = _dot(y_a.astype(BF16), w_cb_ref[...])

            def recurrence():
                z_r = jnp.concatenate([s.pop(("z_r", hd)) for hd in range(LRU_HEADS)], axis=-1)
                z_i = jnp.concatenate([s.pop(("z_i", hd)) for hd in range(LRU_HEADS)], axis=-1)
                r = jax.nn.sigmoid(z_r + ba_ref[...])
                i = jax.nn.sigmoid(z_i + bx_ref[...])
                a = jnp.exp(r * log_a_scale)
                mult = jnp.sqrt(1.0 - a * a)
                if c == 0:
                    row = lax.broadcasted_iota(jnp.int32, (crows, d), 0)
                    seq_start_rows = jnp.where(j == PACK_STEPS, batch, 0)
                    mult = jnp.where(row < seq_start_rows, 1.0, mult)
                u = mult * (i * s.pop("xl"))
                h_t = lru_state["h_t"]
                for t in range(STEPS_PER_CHUNK):
                    rs = slice(t * batch, (t + 1) * batch)
                    h_t = a[rs, :] * h_t + u[rs, :]
                    hs_buf[r0 + t * batch:r0 + (t + 1) * batch, :] = h_t
                lru_state["h_t"] = h_t
                s["y_b"] = (hs_buf[r0:r0 + crows, :] * _gelu_tanh(s.pop("l_y"))).astype(BF16)

            def lru_branch_and_merge():
                br_b = _dot(s.pop("y_b"), w_lb_ref[...])
                merged = (jax.nn.sigmoid(s.pop("g_conv")) * s.pop("br_a")
                          + jax.nn.sigmoid(s.pop("g_lru")) * br_b)
                s["merged"] = merged.astype(BF16)

            def output():
                mix = _dot(s.pop("merged"), w_out_ref[...])
                o_ref[r0:r0 + crows, :] = s.pop("x") + _rmsnorm(mix, g_post_ref[...])

            head = [prenorm, proj("l_x", 3), proj("c_c", 1)]
            body = [lru_conv, gate("z_r", wa_ref, 0), proj("c_x", 2), gate("z_i", wx_ref, 0),
                    proj("c_b", 0), gate("z_r", wa_ref, 1), proj("l_y", 4), gate("z_i", wx_ref, 1),
                    proj("g_conv", 5), gate("z_r", wa_ref, 2), proj("g_lru", 6),
                    gate("z_i", wx_ref, 2), conv_branch, gate("z_r", wa_ref, 3),
                    gate("z_i", wx_ref, 3), recurrence]
            return head, body, lru_branch_and_merge, output

        plans = [chunk_steps(c) for c in range(n_chunks)]
        for step in plans[0][0]:
            step()
        for c in range(n_chunks):
            _, body, lru_branch_and_merge, output = plans[c]
            *front, gate_r3, gate_i3, recurrence = body
            start = 0 if c == 0 else 6
            for step in front[start:]:
                step()
            if c + 1 < n_chunks:
                prenorm, proj_lx, proj_cc = plans[c + 1][0]
                next_front = plans[c + 1][1][:6]
                prenorm()
                gate_r3()
                proj_lx()
                gate_i3()
                proj_cc()
                recurrence()
                for step in next_front[:3]:
                    step()
                lru_branch_and_merge()
                for step in next_front[3:]:
                    step()
                output()
            else:
                gate_r3()
                gate_i3()
                recurrence()
                lru_branch_and_merge()
                output()
        h_t = lru_state["h_t"]

        h_carry[...] = h_t
        cv_buf[0:cv_tail, :] = cv_buf[rows:rows + cv_tail, :]
        lx_buf[0:lx_tail, :] = lx_buf[rows:rows + lx_tail, :]


def _ffn_kernel(x_ref, g_pre_ref, g_post_ref, w_up_f32, cw_ref, cb_ref, w_down_f32,
                o_ref, up_buf, f_buf, w_up_ref, w_down_ref):
    rows, d = x_ref.shape
    batch, steps, _ = o_ref.shape
    crows = STEPS_PER_CHUNK * batch
    d_ff = f_buf.shape[1]
    j = pl.program_id(0)
    tail = up_buf.shape[0] - rows

    @pl.when(j == 0)
    def _reset_state():
        up_buf[0:tail, :] = jnp.zeros((tail, up_buf.shape[1]), F32)

    @pl.when(j < PACK_STEPS)
    def _pack_weights():
        _pack_rows(j, w_up_f32, w_up_ref)
        _pack_rows(j, w_down_f32, w_down_ref)

    @pl.when(j >= PACK_STEPS)
    def _tile():
        for c in range(steps // STEPS_PER_CHUNK):
            r0 = c * crows
            t0 = c * STEPS_PER_CHUNK
            x = x_ref[r0:r0 + crows, :]
            h = _rmsnorm(x, g_pre_ref[...]).astype(BF16)

            def conv_up(col):
                cols = slice(col, col + FFN_CHUNK)
                up = _dot(h, w_up_ref[:, cols])
                up_buf[tail + r0:tail + r0 + crows, cols] = up
                return _causal_conv(up_buf, up, cw_ref, cols, r0, crows, batch) + cb_ref[:, cols]

            for n in range(d_ff // FFN_CHUNK):
                gate = conv_up(n * FFN_CHUNK)
                val = conv_up(d_ff + n * FFN_CHUNK)
                f_buf[r0:r0 + crows, n * FFN_CHUNK:(n + 1) * FFN_CHUNK] = (
                    _gelu_tanh(gate) * val).astype(BF16)

            out = _dot(f_buf[r0:r0 + crows, :], w_down_ref[...])
            res = x + _rmsnorm(out, g_post_ref[...])
            o_ref[:, t0:t0 + STEPS_PER_CHUNK, :] = jnp.swapaxes(
                res.reshape(STEPS_PER_CHUNK, batch, d), 0, 1)

        up_buf[0:tail, :] = up_buf[rows:rows + tail, :]


def _full(a):
    return pl.BlockSpec(a.shape, lambda j: (0,) * a.ndim)


def _weight_rows(w):
    return pl.BlockSpec((w.shape[0] // PACK_STEPS, w.shape[1]),
                        lambda j: (jnp.minimum(j, PACK_STEPS - 1), 0))


def _packed(w):
    return pltpu.VMEM((w.shape[0] // 2, w.shape[1]), jnp.uint32)


def _tile_index(j):
    return jnp.maximum(j - PACK_STEPS, 0)


_COMPILER_PARAMS = pltpu.CompilerParams(
    dimension_semantics=("arbitrary",), vmem_limit_bytes=VMEM_LIMIT_BYTES)


def _mixer(x, g_pre, g_post, w_in, csw, w_cb, lcw, lcb, wa, ba, wx, bx, lam, w_lb, w_out):
    batch, s, d = x.shape
    rows = STEPS_PER_TILE * batch
    args = (g_pre, g_post, w_in, csw, w_cb, lcw, lcb, wa, ba, wx, bx, lam, w_lb, w_out)
    return pl.pallas_call(
        _mixer_kernel,
        grid=(PACK_STEPS + s // STEPS_PER_TILE,),
        in_specs=[pl.BlockSpec((batch, STEPS_PER_TILE, d), lambda j: (0, _tile_index(j), 0)),
                  _full(g_pre), _full(g_post), _weight_rows(w_in), _full(csw), _weight_rows(w_cb),
                  _full(lcw), _full(lcb), _weight_rows(wa), _full(ba), _weight_rows(wx),
                  _full(bx), _full(lam), _weight_rows(w_lb), _weight_rows(w_out)],
        out_specs=pl.BlockSpec((rows, d), lambda j: (_tile_index(j), 0)),
        out_shape=jax.ShapeDtypeStruct((s * batch, d), x.dtype),
        scratch_shapes=[
            pltpu.VMEM((rows + (csw.shape[0] - 1) * batch, d), F32),
            pltpu.VMEM((rows + (lcw.shape[0] - 1) * batch, d), F32),
            pltpu.VMEM((rows, d), F32),
            pltpu.VMEM((batch, d), F32),
        ] + [_packed(w) for w in (w_in, w_cb, wa, wx, w_lb, w_out)],
        compiler_params=_COMPILER_PARAMS,
        name="mixer",
    )(x, *args)


def _convffn(x, batch, g_pre, g_post, w_up, cw, cb, w_down):
    n, d = x.shape
    s = n // batch
    d_ff = w_down.shape[0]
    rows = STEPS_PER_TILE * batch
    return pl.pallas_call(
        _ffn_kernel,
        grid=(PACK_STEPS + s // STEPS_PER_TILE,),
        in_specs=[pl.BlockSpec((rows, d), lambda j: (_tile_index(j), 0)),
                  _full(g_pre), _full(g_post), _weight_rows(w_up), _full(cw), _full(cb),
                  _weight_rows(w_down)],
        out_specs=pl.BlockSpec((batch, STEPS_PER_TILE, d), lambda j: (0, _tile_index(j), 0)),
        out_shape=jax.ShapeDtypeStruct((batch, s, d), x.dtype),
        scratch_shapes=[
            pltpu.VMEM((rows + (cw.shape[0] - 1) * batch, 2 * d_ff), F32),
            pltpu.VMEM((rows, d_ff), BF16),
            _packed(w_up),
            _packed(w_down),
        ],
        compiler_params=_COMPILER_PARAMS,
        name="convffn",
    )(x, g_pre, g_post, w_up, cw, cb, w_down)


def kernel(x, norm_mix_pre, norm_mix_post, norm_ffn_pre, norm_ffn_post, w_in, conv_short_w, w_conv_branch, lru_conv_w, lru_conv_b, lru_wa, lru_ba, lru_wx, lru_bx, lru_lambda, w_lru_branch, w_out, ffn_w_up, ffn_conv_w, ffn_conv_b, ffn_w_down):
    depth = w_in.shape[0]
    bsz, s, d = x.shape
    assert bsz == SUBLANES and s % STEPS_PER_TILE == 0 and STEPS_PER_TILE % STEPS_PER_CHUNK == 0
    assert ffn_w_down.shape[1] % FFN_CHUNK == 0
    for l in range(depth):
        xt = _mixer(
            x, norm_mix_pre[l][None], norm_mix_post[l][None], w_in[l],
            conv_short_w[l], w_conv_branch[l], lru_conv_w[l], lru_conv_b[l][None],
            lru_wa[l].reshape(d, -1), lru_ba[l].reshape(1, d),
            lru_wx[l].reshape(d, -1), lru_bx[l].reshape(1, d),
            lru_lambda[l][None], w_lru_branch[l], w_out[l])
        x = _convffn(
            xt, bsz, norm_ffn_pre[l][None], norm_ffn_post[l][None], ffn_w_up[l],
            ffn_conv_w[l], ffn_conv_b[l][None], ffn_w_down[l])
    return x
```

```python
import math

import jax
import jax.numpy as jnp
from jax import lax
from jax.experimental import pallas as pl
from jax.experimental.pallas import tpu as pltpu

F32 = jnp.float32
BF16 = jnp.bfloat16

LRU_HEADS = 4
LRU_C = 8.0
RMS_EPS = 1e-6
SUBLANES = 8
STEPS_PER_TILE = 64
STEPS_PER_CHUNK = 32
FFN_CHUNK = 512
PACK_STEPS = 8
VMEM_LIMIT_BYTES = 60 * 1024 * 1024


def _rmsnorm(x, g):
    ms = jnp.mean(x * x, axis=-1, keepdims=True)
    return x * lax.rsqrt(ms + RMS_EPS) * g


def _gelu_tanh(x):
    c = math.sqrt(2.0 / math.pi)
    return 0.5 * x * (1.0 + jnp.tanh(c * (x + 0.044715 * (x * x * x))))


def _log_sigmoid(x):
    return jnp.minimum(x, 0.0) - jnp.log1p(jnp.exp(-jnp.abs(x)))


def _pack_rows(step, src_ref, dst_ref):
    kb = src_ref.shape[0] // 2
    start = pl.multiple_of(step * kb, kb)
    dst_ref[pl.ds(start, kb), :] = pltpu.bitcast(src_ref[...].astype(BF16), jnp.uint32)


def _dot(a, w_packed):
    w = pltpu.bitcast(w_packed, BF16)
    return jnp.dot(a, w, preferred_element_type=F32)


def _causal_conv(buf, cur, w_ref, cols, row0, nrows, batch):
    k_width = w_ref.shape[0]
    acc = cur * w_ref[k_width - 1:k_width, cols]
    for k in range(k_width - 1):
        start = row0 + k * batch
        acc = acc + buf[start:start + nrows, cols] * w_ref[k:k + 1, cols]
    return acc


def _mixer_kernel(x_ref, g_pre_ref, g_post_ref, w_in_f32, csw_ref, w_cb_f32, lcw_ref, lcb_ref,
                  wa_f32, ba_ref, wx_f32, bx_ref, lam_ref, w_lb_f32, w_out_f32,
                  ffn_w_up_f32, ffn_w_down_f32,
                  o_ref, ffn_w_up_packed, ffn_w_down_packed, cv_buf, lx_buf, hs_buf, h_carry,
                  w_in_ref, w_cb_ref, wa_ref, wx_ref, w_lb_ref, w_out_ref):
    batch, steps, d = x_ref.shape
    rows = steps * batch
    crows = STEPS_PER_CHUNK * batch
    j = pl.program_id(0)
    cv_tail = cv_buf.shape[0] - rows
    lx_tail = lx_buf.shape[0] - rows
    all_cols = slice(0, d)
    dh = d // LRU_HEADS

    @pl.when(j == 0)
    def _reset_state():
        cv_buf[0:cv_tail, :] = jnp.zeros((cv_tail, d), F32)
        lx_buf[0:lx_tail, :] = jnp.zeros((lx_tail, d), F32)
        h_carry[...] = jnp.zeros_like(h_carry)

    @pl.when(j < PACK_STEPS)
    def _pack_weights():
        for src, dst in ((w_in_f32, w_in_ref), (w_cb_f32, w_cb_ref), (wa_f32, wa_ref),
                         (wx_f32, wx_ref), (w_lb_f32, w_lb_ref), (w_out_f32, w_out_ref)):
            _pack_rows(j, src, dst)

    @pl.when(j >= PACK_STEPS)
    def _tile():
        log_a_scale = LRU_C * _log_sigmoid(lam_ref[...])
        n_chunks = steps // STEPS_PER_CHUNK
        state = [{} for _ in range(n_chunks)]

        lru_state = {"h_t": h_carry[...]}

        def chunk_steps(c):
            s, r0, t0 = state[c], c * crows, c * STEPS_PER_CHUNK

            def prenorm():
                x = jnp.swapaxes(x_ref[:, t0:t0 + STEPS_PER_CHUNK, :], 0, 1).reshape(crows, d)
                s["x"] = x
                s["h"] = _rmsnorm(x, g_pre_ref[...]).astype(BF16)

            def proj(name, k):
                def run():
                    s[name] = _dot(s["h"], w_in_ref[:, k * d:(k + 1) * d])
                return run

            def lru_conv():
                l_x = s.pop("l_x")
                lx_buf[lx_tail + r0:lx_tail + r0 + crows, :] = l_x
                xl = _causal_conv(lx_buf, l_x, lcw_ref, all_cols, r0, crows, batch) + lcb_ref[...]
                s["xl"], s["xl_b"] = xl, xl.astype(BF16)

            def gate(name, w_ref, hd):
                def run():
                    s[name, hd] = _dot(s["xl_b"][:, hd * dh:(hd + 1) * dh],
                                       w_ref[hd * dh // 2:(hd + 1) * dh // 2, :])
                return run

            def conv_branch():
                v = s.pop("c_c") * s.pop("c_x")
                cv_buf[cv_tail + r0:cv_tail + r0 + crows, :] = v
                y_a = s.pop("c_b") * _causal_conv(cv_buf, v, csw_ref, all_cols, r0, crows, batch)
                s["br_a"] = _dot(y_a.astype(BF16), w_cb_ref[...])

            def recurrence():
                z_r = jnp.concatenate([s.pop(("z_r", hd)) for hd in range(LRU_HEADS)], axis=-1)
                z_i = jnp.concatenate([s.pop(("z_i", hd)) for hd in range(LRU_HEADS)], axis=-1)
                r = jax.nn.sigmoid(z_r + ba_ref[...])
                i = jax.nn.sigmoid(z_i + bx_ref[...])
                a = jnp.exp(r * log_a_scale)
                mult = jnp.sqrt(1.0 - a * a)
                if c == 0:
                    row = lax.broadcasted_iota(jnp.int32, (crows, d), 0)
                    seq_start_rows = jnp.where(j == PACK_STEPS, batch, 0)
                    mult = jnp.where(row < seq_start_rows, 1.0, mult)
                u = mult * (i * s.pop("xl"))
                h_t = lru_state["h_t"]
                for t in range(STEPS_PER_CHUNK):
                    rs = slice(t * batch, (t + 1) * batch)
                    h_t = a[rs, :] * h_t + u[rs, :]
                    hs_buf[r0 + t * batch:r0 + (t + 1) * batch, :] = h_t
                lru_state["h_t"] = h_t
                s["y_b"] = (hs_buf[r0:r0 + crows, :] * _gelu_tanh(s.pop("l_y"))).astype(BF16)

            def lru_branch_and_merge():
                br_b = _dot(s.pop("y_b"), w_lb_ref[...])
                merged = (jax.nn.sigmoid(s.pop("g_conv")) * s.pop("br_a")
                          + jax.nn.sigmoid(s.pop("g_lru")) * br_b)
                s["merged"] = merged.astype(BF16)

            def output():
                mix = _dot(s.pop("merged"), w_out_ref[...])
                o_ref[r0:r0 + crows, :] = s.pop("x") + _rmsnorm(mix, g_post_ref[...])

            head = [prenorm, proj("l_x", 3), proj("c_c", 1)]
            body = [lru_conv, gate("z_r", wa_ref, 0), proj("c_x", 2), gate("z_i", wx_ref, 0),
                    proj("c_b", 0), gate("z_r", wa_ref, 1), proj("l_y", 4), gate("z_i", wx_ref, 1),
                    proj("g_conv", 5), gate("z_r", wa_ref, 2), proj("g_lru", 6),
                    gate("z_i", wx_ref, 2), conv_branch, gate("z_r", wa_ref, 3),
                    gate("z_i", wx_ref, 3), recurrence]
            return head, body, lru_branch_and_merge, output

        plans = [chunk_steps(c) for c in range(n_chunks)]
        for step in plans[0][0]:
            step()
        for c in range(n_chunks):
            _, body, lru_branch_and_merge, output = plans[c]
            *front, gate_r3, gate_i3, recurrence = body
            start = 0 if c == 0 else 6
            for step in front[start:]:
                step()
            if c + 1 < n_chunks:
                prenorm, proj_lx, proj_cc = plans[c + 1][0]
                next_front = plans[c + 1][1][:6]
                prenorm()
                gate_r3()
                proj_lx()
                gate_i3()
                proj_cc()
                recurrence()
                for step in next_front[:3]:
                    step()
                lru_branch_and_merge()
                for step in next_front[3:]:
                    step()
                output()
            else:
                gate_r3()
                gate_i3()
                recurrence()
                lru_branch_and_merge()
                output()
        h_t = lru_state["h_t"]

        for src, dst in ((ffn_w_up_f32, ffn_w_up_packed), (ffn_w_down_f32, ffn_w_down_packed)):
            dst[...] = pltpu.bitcast(src[...].astype(BF16), jnp.uint32)
        h_carry[...] = h_t
        cv_buf[0:cv_tail, :] = cv_buf[rows:rows + cv_tail, :]
        lx_buf[0:lx_tail, :] = lx_buf[rows:rows + lx_tail, :]


def _ffn_kernel(x_ref, g_pre_ref, g_post_ref, w_up_ref, cw_ref, cb_ref, w_down_ref,
                o_ref, up_buf, f_buf):
    rows, d = x_ref.shape
    batch, steps, _ = o_ref.shape
    crows = STEPS_PER_CHUNK * batch
    d_ff = f_buf.shape[1]
    j = pl.program_id(0)
    tail = up_buf.shape[0] - rows

    @pl.when(j == 0)
    def _reset_state():
        up_buf[0:tail, :] = jnp.zeros((tail, up_buf.shape[1]), F32)

    for c in range(steps // STEPS_PER_CHUNK):
        r0 = c * crows
        t0 = c * STEPS_PER_CHUNK
        x = x_ref[r0:r0 + crows, :]
        h = _rmsnorm(x, g_pre_ref[...]).astype(BF16)

        def conv_up(col):
            cols = slice(col, col + FFN_CHUNK)
            up = _dot(h, w_up_ref[:, cols])
            up_buf[tail + r0:tail + r0 + crows, cols] = up
            return _causal_conv(up_buf, up, cw_ref, cols, r0, crows, batch) + cb_ref[:, cols]

        for n in range(d_ff // FFN_CHUNK):
            gate = conv_up(n * FFN_CHUNK)
            val = conv_up(d_ff + n * FFN_CHUNK)
            f_buf[r0:r0 + crows, n * FFN_CHUNK:(n + 1) * FFN_CHUNK] = (
                _gelu_tanh(gate) * val).astype(BF16)

        out = _dot(f_buf[r0:r0 + crows, :], w_down_ref[...])
        res = x + _rmsnorm(out, g_post_ref[...])
        o_ref[:, t0:t0 + STEPS_PER_CHUNK, :] = jnp.swapaxes(
            res.reshape(STEPS_PER_CHUNK, batch, d), 0, 1)

    up_buf[0:tail, :] = up_buf[rows:rows + tail, :]


def _full(a):
    return pl.BlockSpec(a.shape, lambda j: (0,) * a.ndim)


def _weight_rows(w):
    return pl.BlockSpec((w.shape[0] // PACK_STEPS, w.shape[1]),
                        lambda j: (jnp.minimum(j, PACK_STEPS - 1), 0))


def _packed(w):
    return pltpu.VMEM((w.shape[0] // 2, w.shape[1]), jnp.uint32)


def _tile_index(j):
    return jnp.maximum(j - PACK_STEPS, 0)


_COMPILER_PARAMS = pltpu.CompilerParams(
    dimension_semantics=("arbitrary",), vmem_limit_bytes=VMEM_LIMIT_BYTES)


def _mixer(x, g_pre, g_post, w_in, csw, w_cb, lcw, lcb, wa, ba, wx, bx, lam, w_lb, w_out,
           ffn_weights):
    batch, s, d = x.shape
    rows = STEPS_PER_TILE * batch
    n_tiles = s // STEPS_PER_TILE
    args = (g_pre, g_post, w_in, csw, w_cb, lcw, lcb, wa, ba, wx, bx, lam, w_lb, w_out)

    def tile_rows(w, pack):
        return pl.BlockSpec((w.shape[0] // n_tiles // pack, w.shape[1]),
                            lambda j: (_tile_index(j), 0))

    return pl.pallas_call(
        _mixer_kernel,
        grid=(PACK_STEPS + n_tiles,),
        in_specs=[pl.BlockSpec((batch, STEPS_PER_TILE, d), lambda j: (0, _tile_index(j), 0)),
                  _full(g_pre), _full(g_post), _weight_rows(w_in), _full(csw), _weight_rows(w_cb),
                  _full(lcw), _full(lcb), _weight_rows(wa), _full(ba), _weight_rows(wx),
                  _full(bx), _full(lam), _weight_rows(w_lb), _weight_rows(w_out)]
        + [tile_rows(w, 1) for w in ffn_weights],
        out_specs=[pl.BlockSpec((rows, d), lambda j: (_tile_index(j), 0))]
        + [tile_rows(w, 2) for w in ffn_weights],
        out_shape=[jax.ShapeDtypeStruct((s * batch, d), x.dtype)]
        + [jax.ShapeDtypeStruct((w.shape[0] // 2, w.shape[1]), jnp.uint32) for w in ffn_weights],
        scratch_shapes=[
            pltpu.VMEM((rows + (csw.shape[0] - 1) * batch, d), F32),
            pltpu.VMEM((rows + (lcw.shape[0] - 1) * batch, d), F32),
            pltpu.VMEM((rows, d), F32),
            pltpu.VMEM((batch, d), F32),
        ] + [_packed(w) for w in (w_in, w_cb, wa, wx, w_lb, w_out)],
        compiler_params=_COMPILER_PARAMS,
        name="mixer",
    )(x, *args, *ffn_weights)


def _convffn(x, batch, g_pre, g_post, w_up_packed, cw, cb, w_down_packed):
    n, d = x.shape
    s = n // batch
    d_ff = 2 * w_down_packed.shape[0]
    rows = STEPS_PER_TILE * batch
    return pl.pallas_call(
        _ffn_kernel,
        grid=(s // STEPS_PER_TILE,),
        in_specs=[pl.BlockSpec((rows, d), lambda j: (j, 0)),
                  _full(g_pre), _full(g_post), _full(w_up_packed), _full(cw), _full(cb),
                  _full(w_down_packed)],
        out_specs=pl.BlockSpec((batch, STEPS_PER_TILE, d), lambda j: (0, j, 0)),
        out_shape=jax.ShapeDtypeStruct((batch, s, d), x.dtype),
        scratch_shapes=[
            pltpu.VMEM((rows + (cw.shape[0] - 1) * batch, 2 * d_ff), F32),
            pltpu.VMEM((rows, d_ff), BF16),
        ],
        compiler_params=_COMPILER_PARAMS,
        name="convffn",
    )(x, g_pre, g_post, w_up_packed, cw, cb, w_down_packed)


def kernel(x, norm_mix_pre, norm_mix_post, norm_ffn_pre, norm_ffn_post, w_in, conv_short_w, w_conv_branch, lru_conv_w, lru_conv_b, lru_wa, lru_ba, lru_wx, lru_bx, lru_lambda, w_lru_branch, w_out, ffn_w_up, ffn_conv_w, ffn_conv_b, ffn_w_down):
    depth = w_in.shape[0]
    bsz, s, d = x.shape
    assert bsz == SUBLANES and s % STEPS_PER_TILE == 0 and STEPS_PER_TILE % STEPS_PER_CHUNK == 0
    assert ffn_w_down.shape[1] % FFN_CHUNK == 0
    for l in range(depth):
        xt, w_up_packed, w_down_packed = _mixer(
            x, norm_mix_pre[l][None], norm_mix_post[l][None], w_in[l],
            conv_short_w[l], w_conv_branch[l], lru_conv_w[l], lru_conv_b[l][None],
            lru_wa[l].reshape(d, -1), lru_ba[l].reshape(1, d),
            lru_wx[l].reshape(d, -1), lru_bx[l].reshape(1, d),
            lru_lambda[l][None], w_lru_branch[l], w_out[l], (ffn_w_up[l], ffn_w_down[l]))
        x = _convffn(
            xt, bsz, norm_ffn_pre[l][None], norm_ffn_post[l][None], w_up_packed,
            ffn_conv_w[l], ffn_conv_b[l][None], w_down_packed)
    return x
```

```python
import math

import jax
import jax.numpy as jnp
from jax import lax
from jax.experimental import pallas as pl
from jax.experimental.pallas import tpu as pltpu

F32 = jnp.float32
BF16 = jnp.bfloat16

LRU_HEADS = 4
LRU_C = 8.0
RMS_EPS = 1e-6
SUBLANES = 8
STEPS_PER_TILE = 64
STEPS_PER_CHUNK = 32
FFN_CHUNK = 512
PACK_STEPS = 8
VMEM_LIMIT_BYTES = 60 * 1024 * 1024


def _rmsnorm(x, g):
    ms = jnp.mean(x * x, axis=-1, keepdims=True)
    return x * lax.rsqrt(ms + RMS_EPS) * g


def _gelu_tanh(x):
    c = math.sqrt(2.0 / math.pi)
    return 0.5 * x * (1.0 + jnp.tanh(c * (x + 0.044715 * (x * x * x))))


def _log_sigmoid(x):
    return jnp.minimum(x, 0.0) - jnp.log1p(jnp.exp(-jnp.abs(x)))


def _pack_rows(step, src_ref, dst_ref):
    kb = src_ref.shape[0] // 2
    start = pl.multiple_of(step * kb, kb)
    dst_ref[pl.ds(start, kb), :] = pltpu.bitcast(src_ref[...].astype(BF16), jnp.uint32)


def _dot(a, w_packed):
    w = pltpu.bitcast(w_packed, BF16)
    return jnp.dot(a, w, preferred_element_type=F32)


def _causal_conv(buf, cur, w_ref, cols, row0, nrows, batch):
    k_width = w_ref.shape[0]
    acc = cur * w_ref[k_width - 1:k_width, cols]
    for k in range(k_width - 1):
        start = row0 + k * batch
        acc = acc + buf[start:start + nrows, cols] * w_ref[k:k + 1, cols]
    return acc


def _mixer_kernel(x_ref, g_pre_ref, g_post_ref, w_in_f32, csw_ref, w_cb_f32, lcw_ref, lcb_ref,
                  wa_f32, ba_ref, wx_f32, bx_ref, lam_ref, w_lb_f32, w_out_f32,
                  ffn_w_up_f32, ffn_w_down_f32,
                  o_ref, ffn_w_up_packed, ffn_w_down_packed, cv_buf, lx_buf, hs_buf, h_carry,
                  w_in_ref, w_cb_ref, wa_ref, wx_ref, w_lb_ref, w_out_ref):
    batch, steps, d = x_ref.shape
    rows = steps * batch
    crows = STEPS_PER_CHUNK * batch
    j = pl.program_id(0)
    cv_tail = cv_buf.shape[0] - rows
    lx_tail = lx_buf.shape[0] - rows
    all_cols = slice(0, d)
    dh = d // LRU_HEADS

    @pl.when(j == 0)
    def _reset_state():
        cv_buf[0:cv_tail, :] = jnp.zeros((cv_tail, d), F32)
        lx_buf[0:lx_tail, :] = jnp.zeros((lx_tail, d), F32)
        h_carry[...] = jnp.zeros_like(h_carry)

    @pl.when(j < PACK_STEPS)
    def _pack_weights():
        for src, dst in ((w_in_f32, w_in_ref), (w_cb_f32, w_cb_ref), (wa_f32, wa_ref),
                         (wx_f32, wx_ref), (w_lb_f32, w_lb_ref), (w_out_f32, w_out_ref)):
            _pack_rows(j, src, dst)

    @pl.when(j >= PACK_STEPS)
    def _tile():
        log_a_scale = LRU_C * _log_sigmoid(lam_ref[...])
        n_chunks = steps // STEPS_PER_CHUNK
        state = [{} for _ in range(n_chunks)]

        lru_state = {"h_t": h_carry[...]}

        def chunk_steps(c):
            s, r0, t0 = state[c], c * crows, c * STEPS_PER_CHUNK

            def prenorm():
                x = jnp.swapaxes(x_ref[:, t0:t0 + STEPS_PER_CHUNK, :], 0, 1).reshape(crows, d)
                s["x"] = x
                s["h"] = _rmsnorm(x, g_pre_ref[...]).astype(BF16)

            def proj(name, k):
                def run():
                    s[name] = _dot(s["h"], w_in_ref[:, k * d:(k + 1) * d])
                return run

            def lru_conv():
                l_x = s.pop("l_x")
                lx_buf[lx_tail + r0:lx_tail + r0 + crows, :] = l_x
                xl = _causal_conv(lx_buf, l_x, lcw_ref, all_cols, r0, crows, batch) + lcb_ref[...]
                s["xl"], s["xl_b"] = xl, xl.astype(BF16)

            def gate(name, w_ref, hd):
                def run():
                    s[name, hd] = _dot(s["xl_b"][:, hd * dh:(hd + 1) * dh],
                                       w_ref[hd * dh // 2:(hd + 1) * dh // 2, :])
                return run

            def conv_branch():
                v = s.pop("c_c") * s.pop("c_x")
                cv_buf[cv_tail + r0:cv_tail + r0 + crows, :] = v
                y_a = s.pop("c_b") * _causal_conv(cv_buf, v, csw_ref, all_cols, r0, crows, batch)
                s["br_a"] = _dot(y_a.astype(BF16), w_cb_ref[...])

            def recurrence():
                z_r = jnp.concatenate([s.pop(("z_r", hd)) for hd in range(LRU_HEADS)], axis=-1)
                z_i = jnp.concatenate([s.pop(("z_i", hd)) for hd in range(LRU_HEADS)], axis=-1)
                r = jax.nn.sigmoid(z_r + ba_ref[...])
                i = jax.nn.sigmoid(z_i + bx_ref[...])
                a = jnp.exp(r * log_a_scale)
                mult = jnp.sqrt(1.0 - a * a)
                if c == 0:
                    row = lax.broadcasted_iota(jnp.int32, (crows, d), 0)
                    seq_start_rows = jnp.where(j == PACK_STEPS, batch, 0)
                    mult = jnp.where(row < seq_start_rows, 1.0, mult)
                u = mult * (i * s.pop("xl"))
                h_t = lru_state["h_t"]
                for t in range(STEPS_PER_CHUNK):
                    rs = slice(t * batch, (t + 1) * batch)
                    h_t = a[rs, :] * h_t + u[rs, :]
                    hs_buf[r0 + t * batch:r0 + (t + 1) * batch, :] = h_t
                lru_state["h_t"] = h_t
                s["y_b"] = (hs_buf[r0:r0 + crows, :] * _gelu_tanh(s.pop("l_y"))).astype(BF16)

            def lru_branch_and_merge():
                br_b = _dot(s.pop("y_b"), w_lb_ref[...])
                merged = (jax.nn.sigmoid(s.pop("g_conv")) * s.pop("br_a")
                          + jax.nn.sigmoid(s.pop("g_lru")) * br_b)
                s["merged"] = merged.astype(BF16)

            def output():
                mix = _dot(s.pop("merged"), w_out_ref[...])
                o_ref[r0:r0 + crows, :] = s.pop("x") + _rmsnorm(mix, g_post_ref[...])

            head = [prenorm, proj("l_x", 3), proj("c_c", 1)]
            body = [lru_conv, gate("z_r", wa_ref, 0), proj("c_x", 2), gate("z_i", wx_ref, 0),
                    proj("c_b", 0), gate("z_r", wa_ref, 1), proj("l_y", 4), gate("z_i", wx_ref, 1),
                    proj("g_conv", 5), gate("z_r", wa_ref, 2), proj("g_lru", 6),
                    gate("z_i", wx_ref, 2), conv_branch, gate("z_r", wa_ref, 3),
                    gate("z_i", wx_ref, 3), recurrence]
            return head, body, lru_branch_and_merge, output

        plans = [chunk_steps(c) for c in range(n_chunks)]
        for step in plans[0][0]:
            step()
        for c in range(n_chunks):
            _, body, lru_branch_and_merge, output = plans[c]
            *front, gate_r3, gate_i3, recurrence = body
            start = 0 if c == 0 else 6
            for step in front[start:]:
                step()
            if c + 1 < n_chunks:
                prenorm, proj_lx, proj_cc = plans[c + 1][0]
                next_front = plans[c + 1][1][:6]
                prenorm()
                gate_r3()
                proj_lx()
                gate_i3()
                proj_cc()
                recurrence()
                for step in next_front[:3]:
                    step()
                lru_branch_and_merge()
                for step in next_front[3:]:
                    step()
                output()
            else:
                gate_r3()
                gate_i3()
                recurrence()
                lru_branch_and_merge()
                output()
        h_t = lru_state["h_t"]

        for src, dst in ((ffn_w_up_f32, ffn_w_up_packed), (ffn_w_down_f32, ffn_w_down_packed)):
            dst[...] = pltpu.bitcast(src[...].astype(BF16), jnp.uint32)
        h_carry[...] = h_t
        cv_buf[0:cv_tail, :] = cv_buf[rows:rows + cv_tail, :]
        lx_buf[0:lx_tail, :] = lx_buf[rows:rows + lx_tail, :]


def _ffn_kernel(x_ref, g_pre_ref, g_post_ref, w_up_ref, cw_ref, cb_ref, w_down_ref,
                o_ref, up_buf, f_buf):
    rows, d = x_ref.shape
    batch, steps, _ = o_ref.shape
    crows = STEPS_PER_CHUNK * batch
    d_ff = f_buf.shape[1]
    j = pl.program_id(0)
    tail = up_buf.shape[0] - rows

    @pl.when(j == 0)
    def _reset_state():
        up_buf[0:tail, :] = jnp.zeros((tail, up_buf.shape[1]), F32)

    for c in range(steps // STEPS_PER_CHUNK):
        r0 = c * crows
        t0 = c * STEPS_PER_CHUNK
        x = x_ref[r0:r0 + crows, :]
        h = _rmsnorm(x, g_pre_ref[...]).astype(BF16)

        def conv_up(col):
            cols = slice(col, col + FFN_CHUNK)
            up = _dot(h, w_up_ref[:, cols])
            up_buf[tail + r0:tail + r0 + crows, cols] = up
            return _causal_conv(up_buf, up, cw_ref, cols, r0, crows, batch) + cb_ref[:, cols]

        for n in range(d_ff // FFN_CHUNK):
            gate = conv_up(n * FFN_CHUNK)
            val = conv_up(d_ff + n * FFN_CHUNK)
            f_buf[r0:r0 + crows, n * FFN_CHUNK:(n + 1) * FFN_CHUNK] = (
                _gelu_tanh(gate) * val).astype(BF16)

        out = _dot(f_buf[r0:r0 + crows, :], w_down_ref[...])
        res = x + _rmsnorm(out, g_post_ref[...])
        o_ref[:, t0:t0 + STEPS_PER_CHUNK, :] = jnp.swapaxes(
            res.reshape(STEPS_PER_CHUNK, batch, d), 0, 1)

    up_buf[0:tail, :] = up_buf[rows:rows + tail, :]


def _full(a):
    return pl.BlockSpec(a.shape, lambda j: (0,) * a.ndim)


def _weight_rows(w):
    assert w.shape[0] % (PACK_STEPS * 2 * SUBLANES) == 0, w.shape
    return pl.BlockSpec((w.shape[0] // PACK_STEPS, w.shape[1]),
                        lambda j: (jnp.minimum(j, PACK_STEPS - 1), 0))


def _packed(w):
    return pltpu.VMEM((w.shape[0] // 2, w.shape[1]), jnp.uint32)


def _tile_index(j):
    return jnp.maximum(j - PACK_STEPS, 0)


_COMPILER_PARAMS = pltpu.CompilerParams(
    dimension_semantics=("arbitrary",), vmem_limit_bytes=VMEM_LIMIT_BYTES)


def _mixer(x, g_pre, g_post, w_in, csw, w_cb, lcw, lcb, wa, ba, wx, bx, lam, w_lb, w_out,
           ffn_weights):
    batch, s, d = x.shape
    rows = STEPS_PER_TILE * batch
    n_tiles = s // STEPS_PER_TILE
    args = (g_pre, g_post, w_in, csw, w_cb, lcw, lcb, wa, ba, wx, bx, lam, w_lb, w_out)

    def tile_rows(w, pack):
        assert w.shape[0] % (n_tiles * 2 * SUBLANES) == 0, (w.shape, n_tiles)
        return pl.BlockSpec((w.shape[0] // n_tiles // pack, w.shape[1]),
                            lambda j: (_tile_index(j), 0))

    return pl.pallas_call(
        _mixer_kernel,
        grid=(PACK_STEPS + n_tiles,),
        in_specs=[pl.BlockSpec((batch, STEPS_PER_TILE, d), lambda j: (0, _tile_index(j), 0)),
                  _full(g_pre), _full(g_post), _weight_rows(w_in), _full(csw), _weight_rows(w_cb),
                  _full(lcw), _full(lcb), _weight_rows(wa), _full(ba), _weight_rows(wx),
                  _full(bx), _full(lam), _weight_rows(w_lb), _weight_rows(w_out)]
        + [tile_rows(w, 1) for w in ffn_weights],
        out_specs=[pl.BlockSpec((rows, d), lambda j: (_tile_index(j), 0))]
        + [tile_rows(w, 2) for w in ffn_weights],
        out_shape=[jax.ShapeDtypeStruct((s * batch, d), x.dtype)]
        + [jax.ShapeDtypeStruct((w.shape[0] // 2, w.shape[1]), jnp.uint32) for w in ffn_weights],
        scratch_shapes=[
            pltpu.VMEM((rows + (csw.shape[0] - 1) * batch, d), F32),
            pltpu.VMEM((rows + (lcw.shape[0] - 1) * batch, d), F32),
            pltpu.VMEM((rows, d), F32),
            pltpu.VMEM((batch, d), F32),
        ] + [_packed(w) for w in (w_in, w_cb, wa, wx, w_lb, w_out)],
        compiler_params=_COMPILER_PARAMS,
        name="mixer",
    )(x, *args, *ffn_weights)


def _convffn(x, batch, g_pre, g_post, w_up_packed, cw, cb, w_down_packed):
    n, d = x.shape
    s = n // batch
    d_ff = 2 * w_down_packed.shape[0]
    rows = STEPS_PER_TILE * batch
    return pl.pallas_call(
        _ffn_kernel,
        grid=(s // STEPS_PER_TILE,),
        in_specs=[pl.BlockSpec((rows, d), lambda j: (j, 0)),
                  _full(g_pre), _full(g_post), _full(w_up_packed), _full(cw), _full(cb),
                  _full(w_down_packed)],
        out_specs=pl.BlockSpec((batch, STEPS_PER_TILE, d), lambda j: (0, j, 0)),
        out_shape=jax.ShapeDtypeStruct((batch, s, d), x.dtype),
        scratch_shapes=[
            pltpu.VMEM((rows + (cw.shape[0] - 1) * batch, 2 * d_ff), F32),
            pltpu.VMEM((rows, d_ff), BF16),
        ],
        compiler_params=_COMPILER_PARAMS,
        name="convffn",
    )(x, g_pre, g_post, w_up_packed, cw, cb, w_down_packed)


def kernel(x, norm_mix_pre, norm_mix_post, norm_ffn_pre, norm_ffn_post, w_in, conv_short_w, w_conv_branch, lru_conv_w, lru_conv_b, lru_wa, lru_ba, lru_wx, lru_bx, lru_lambda, w_lru_branch, w_out, ffn_w_up, ffn_conv_w, ffn_conv_b, ffn_w_down):
    depth = w_in.shape[0]
    bsz, s, d = x.shape
    assert bsz == SUBLANES and s % STEPS_PER_TILE == 0 and STEPS_PER_TILE % STEPS_PER_CHUNK == 0
    assert ffn_w_down.shape[1] % FFN_CHUNK == 0
    for l in range(depth):
        xt, w_up_packed, w_down_packed = _mixer(
            x, norm_mix_pre[l][None], norm_mix_post[l][None], w_in[l],
            conv_short_w[l], w_conv_branch[l], lru_conv_w[l], lru_conv_b[l][None],
            lru_wa[l].reshape(d, -1), lru_ba[l].reshape(1, d),
            lru_wx[l].reshape(d, -1), lru_bx[l].reshape(1, d),
            lru_lambda[l][None], w_lru_branch[l], w_out[l], (ffn_w_up[l], ffn_w_down[l]))
        x = _convffn(
            xt, bsz, norm_ffn_pre[l][None], norm_ffn_post[l][None], w_up_packed,
            ffn_conv_w[l], ffn_conv_b[l][None], w_down_packed)
    return x
```

```python
import math

import jax
import jax.numpy as jnp
from jax import lax
from jax.experimental import pallas as pl
from jax.experimental.pallas import tpu as pltpu

F32 = jnp.float32
BF16 = jnp.bfloat16

LRU_HEADS = 4
LRU_C = 8.0
RMS_EPS = 1e-6
SUBLANES = 8
STEPS_PER_TILE = 64
STEPS_PER_CHUNK = 32
FFN_CHUNK = 512
PACK_STEPS = 8
VMEM_LIMIT_BYTES = 60 * 1024 * 1024


def _rmsnorm(x, g):
    ms = jnp.mean(x * x, axis=-1, keepdims=True)
    return x * lax.rsqrt(ms + RMS_EPS) * g


def _gelu_tanh(x):
    c = math.sqrt(2.0 / math.pi)
    return 0.5 * x * (1.0 + jnp.tanh(c * (x + 0.044715 * (x * x * x))))


def _log_sigmoid(x):
    return jnp.minimum(x, 0.0) - jnp.log1p(jnp.exp(-jnp.abs(x)))


def _pack_rows(step, src_ref, dst_ref):
    kb = src_ref.shape[0] // 2
    start = pl.multiple_of(step * kb, kb)
    dst_ref[pl.ds(start, kb), :] = pltpu.bitcast(src_ref[...].astype(BF16), jnp.uint32)


def _dot(a, w_packed):
    w = pltpu.bitcast(w_packed, BF16)
    return jnp.dot(a, w, preferred_element_type=F32)


def _causal_conv(buf, cur, w_ref, cols, row0, nrows, batch):
    k_width = w_ref.shape[0]
    acc = cur * w_ref[k_width - 1:k_width, cols]
    for k in range(k_width - 1):
        start = row0 + k * batch
        acc = acc + buf[start:start + nrows, cols] * w_ref[k:k + 1, cols]
    return acc


def _mixer_kernel(x_ref, g_pre_ref, g_post_ref, w_in_f32, csw_ref, w_cb_f32, lcw_ref, lcb_ref,
                  wa_f32, ba_ref, wx_f32, bx_ref, lam_ref, w_lb_f32, w_out_f32,
                  ffn_w_up_f32, ffn_w_down_f32,
                  o_ref, ffn_w_up_packed, ffn_w_down_packed, cv_buf, lx_buf, hs_buf, h_carry,
                  w_in_ref, w_cb_ref, wa_ref, wx_ref, w_lb_ref, w_out_ref):
    batch, steps, d = x_ref.shape
    rows = steps * batch
    crows = STEPS_PER_CHUNK * batch
    j = pl.program_id(0)
    cv_tail = cv_buf.shape[0] - rows
    lx_tail = lx_buf.shape[0] - rows
    all_cols = slice(0, d)
    dh = d // LRU_HEADS

    @pl.when(j == 0)
    def _reset_state():
        cv_buf[0:cv_tail, :] = jnp.zeros((cv_tail, d), F32)
        lx_buf[0:lx_tail, :] = jnp.zeros((lx_tail, d), F32)
        h_carry[...] = jnp.zeros_like(h_carry)

    @pl.when(j < PACK_STEPS)
    def _pack_weights():
        for src, dst in ((w_in_f32, w_in_ref), (w_cb_f32, w_cb_ref), (wa_f32, wa_ref),
                         (wx_f32, wx_ref), (w_lb_f32, w_lb_ref), (w_out_f32, w_out_ref)):
            _pack_rows(j, src, dst)

    @pl.when(j >= PACK_STEPS)
    def _tile():
        log_a_scale = LRU_C * _log_sigmoid(lam_ref[...])
        n_chunks = steps // STEPS_PER_CHUNK
        state = [{} for _ in range(n_chunks)]

        lru_state = {"h_t": h_carry[...]}

        def chunk_steps(c):
            s, r0, t0 = state[c], c * crows, c * STEPS_PER_CHUNK

            def prenorm():
                x = jnp.swapaxes(x_ref[:, t0:t0 + STEPS_PER_CHUNK, :], 0, 1).reshape(crows, d)
                s["x"] = x
                s["h"] = _rmsnorm(x, g_pre_ref[...]).astype(BF16)

            def proj(name, k):
                def run():
                    s[name] = _dot(s["h"], w_in_ref[:, k * d:(k + 1) * d])
                return run

            def lru_conv():
                l_x = s.pop("l_x")
                lx_buf[lx_tail + r0:lx_tail + r0 + crows, :] = l_x
                xl = _causal_conv(lx_buf, l_x, lcw_ref, all_cols, r0, crows, batch) + lcb_ref[...]
                s["xl"], s["xl_b"] = xl, xl.astype(BF16)

            def gate(name, w_ref, hd):
                def run():
                    s[name, hd] = _dot(s["xl_b"][:, hd * dh:(hd + 1) * dh],
                                       w_ref[hd * dh // 2:(hd + 1) * dh // 2, :])
                return run

            def conv_branch():
                v = s.pop("c_c") * s.pop("c_x")
                cv_buf[cv_tail + r0:cv_tail + r0 + crows, :] = v
                y_a = s.pop("c_b") * _causal_conv(cv_buf, v, csw_ref, all_cols, r0, crows, batch)
                s["br_a"] = _dot(y_a.astype(BF16), w_cb_ref[...])

            def recurrence():
                z_r = jnp.concatenate([s.pop(("z_r", hd)) for hd in range(LRU_HEADS)], axis=-1)
                z_i = jnp.concatenate([s.pop(("z_i", hd)) for hd in range(LRU_HEADS)], axis=-1)
                r = jax.nn.sigmoid(z_r + ba_ref[...])
                i = jax.nn.sigmoid(z_i + bx_ref[...])
                a = jnp.exp(r * log_a_scale)
                mult = jnp.sqrt(1.0 - a * a)
                if c == 0:
                    row = lax.broadcasted_iota(jnp.int32, (crows, d), 0)
                    seq_start_rows = jnp.where(j == PACK_STEPS, batch, 0)
                    mult = jnp.where(row < seq_start_rows, 1.0, mult)
                u = mult * (i * s.pop("xl"))
                h_t = lru_state["h_t"]
                for t in range(STEPS_PER_CHUNK):
                    rs = slice(t * batch, (t + 1) * batch)
                    h_t = a[rs, :] * h_t + u[rs, :]
                    hs_buf[r0 + t * batch:r0 + (t + 1) * batch, :] = h_t
                lru_state["h_t"] = h_t
                s["y_b"] = (hs_buf[r0:r0 + crows, :] * _gelu_tanh(s.pop("l_y"))).astype(BF16)

            def lru_branch_and_merge():
                br_b = _dot(s.pop("y_b"), w_lb_ref[...])
                merged = (jax.nn.sigmoid(s.pop("g_conv")) * s.pop("br_a")
                          + jax.nn.sigmoid(s.pop("g_lru")) * br_b)
                s["merged"] = merged.astype(BF16)

            def output():
                mix = _dot(s.pop("merged"), w_out_ref[...])
                o_ref[r0:r0 + crows, :] = s.pop("x") + _rmsnorm(mix, g_post_ref[...])

            head = [prenorm, proj("l_x", 3), proj("c_c", 1)]
            body = [lru_conv, gate("z_r", wa_ref, 0), proj("c_x", 2), gate("z_i", wx_ref, 0),
                    proj("c_b", 0), gate("z_r", wa_ref, 1), proj("l_y", 4), gate("z_i", wx_ref, 1),
                    proj("g_conv", 5), gate("z_r", wa_ref, 2), proj("g_lru", 6),
                    gate("z_i", wx_ref, 2), conv_branch, gate("z_r", wa_ref, 3),
                    gate("z_i", wx_ref, 3), recurrence]
            return head, body, lru_branch_and_merge, output

        plans = [chunk_steps(c) for c in range(n_chunks)]
        for step in plans[0][0]:
            step()
        for c in range(n_chunks):
            _, body, lru_branch_and_merge, output = plans[c]
            *front, gate_r3, gate_i3, recurrence = body
            start = 0 if c == 0 else 6
            for step in front[start:]:
                step()
            if c + 1 < n_chunks:
                prenorm, proj_lx, proj_cc = plans[c + 1][0]
                next_front = plans[c + 1][1][:6]
                prenorm()
                gate_r3()
                proj_lx()
                gate_i3()
                proj_cc()
                recurrence()
                for step in next_front[:2]:
                    step()
                lru_branch_and_merge()
                for step in next_front[2:]:
                    step()
                output()
            else:
                gate_r3()
                gate_i3()
                recurrence()
                lru_branch_and_merge()
                output()
        h_t = lru_state["h_t"]

        for src, dst in ((ffn_w_up_f32, ffn_w_up_packed), (ffn_w_down_f32, ffn_w_down_packed)):
            dst[...] = pltpu.bitcast(src[...].astype(BF16), jnp.uint32)
        h_carry[...] = h_t
        cv_buf[0:cv_tail, :] = cv_buf[rows:rows + cv_tail, :]
        lx_buf[0:lx_tail, :] = lx_buf[rows:rows + lx_tail, :]


def _ffn_kernel(x_ref, g_pre_ref, g_post_ref, w_up_ref, cw_ref, cb_ref, w_down_ref,
                o_ref, up_buf, f_buf):
    rows, d = x_ref.shape
    batch, steps, _ = o_ref.shape
    crows = STEPS_PER_CHUNK * batch
    d_ff = f_buf.shape[1]
    j = pl.program_id(0)
    tail = up_buf.shape[0] - rows

    @pl.when(j == 0)
    def _reset_state():
        up_buf[0:tail, :] = jnp.zeros((tail, up_buf.shape[1]), F32)

    for c in range(steps // STEPS_PER_CHUNK):
        r0 = c * crows
        t0 = c * STEPS_PER_CHUNK
        x = x_ref[r0:r0 + crows, :]
        h = _rmsnorm(x, g_pre_ref[...]).astype(BF16)

        def conv_up(col):
            cols = slice(col, col + FFN_CHUNK)
            up = _dot(h, w_up_ref[:, cols])
            up_buf[tail + r0:tail + r0 + crows, cols] = up
            return _causal_conv(up_buf, up, cw_ref, cols, r0, crows, batch) + cb_ref[:, cols]

        for n in range(d_ff // FFN_CHUNK):
            gate = conv_up(n * FFN_CHUNK)
            val = conv_up(d_ff + n * FFN_CHUNK)
            f_buf[r0:r0 + crows, n * FFN_CHUNK:(n + 1) * FFN_CHUNK] = (
                _gelu_tanh(gate) * val).astype(BF16)

        out = _dot(f_buf[r0:r0 + crows, :], w_down_ref[...])
        res = x + _rmsnorm(out, g_post_ref[...])
        o_ref[:, t0:t0 + STEPS_PER_CHUNK, :] = jnp.swapaxes(
            res.reshape(STEPS_PER_CHUNK, batch, d), 0, 1)

    up_buf[0:tail, :] = up_buf[rows:rows + tail, :]


def _full(a):
    return pl.BlockSpec(a.shape, lambda j: (0,) * a.ndim)


def _weight_rows(w):
    assert w.shape[0] % (PACK_STEPS * 2 * SUBLANES) == 0, w.shape
    return pl.BlockSpec((w.shape[0] // PACK_STEPS, w.shape[1]),
                        lambda j: (jnp.minimum(j, PACK_STEPS - 1), 0))


def _packed(w):
    return pltpu.VMEM((w.shape[0] // 2, w.shape[1]), jnp.uint32)


def _tile_index(j):
    return jnp.maximum(j - PACK_STEPS, 0)


_COMPILER_PARAMS = pltpu.CompilerParams(
    dimension_semantics=("arbitrary",), vmem_limit_bytes=VMEM_LIMIT_BYTES)


def _mixer(x, g_pre, g_post, w_in, csw, w_cb, lcw, lcb, wa, ba, wx, bx, lam, w_lb, w_out,
           ffn_weights):
    batch, s, d = x.shape
    rows = STEPS_PER_TILE * batch
    n_tiles = s // STEPS_PER_TILE
    args = (g_pre, g_post, w_in, csw, w_cb, lcw, lcb, wa, ba, wx, bx, lam, w_lb, w_out)

    def tile_rows(w, pack):
        assert w.shape[0] % (n_tiles * 2 * SUBLANES) == 0, (w.shape, n_tiles)
        return pl.BlockSpec((w.shape[0] // n_tiles // pack, w.shape[1]),
                            lambda j: (_tile_index(j), 0))

    return pl.pallas_call(
        _mixer_kernel,
        grid=(PACK_STEPS + n_tiles,),
        in_specs=[pl.BlockSpec((batch, STEPS_PER_TILE, d), lambda j: (0, _tile_index(j), 0)),
                  _full(g_pre), _full(g_post), _weight_rows(w_in), _full(csw), _weight_rows(w_cb),
                  _full(lcw), _full(lcb), _weight_rows(wa), _full(ba), _weight_rows(wx),
                  _full(bx), _full(lam), _weight_rows(w_lb), _weight_rows(w_out)]
        + [tile_rows(w, 1) for w in ffn_weights],
        out_specs=[pl.BlockSpec((rows, d), lambda j: (_tile_index(j), 0))]
        + [tile_rows(w, 2) for w in ffn_weights],
        out_shape=[jax.ShapeDtypeStruct((s * batch, d), x.dtype)]
        + [jax.ShapeDtypeStruct((w.shape[0] // 2, w.shape[1]), jnp.uint32) for w in ffn_weights],
        scratch_shapes=[
            pltpu.VMEM((rows + (csw.shape[0] - 1) * batch, d), F32),
            pltpu.VMEM((rows + (lcw.shape[0] - 1) * batch, d), F32),
            pltpu.VMEM((rows, d), F32),
            pltpu.VMEM((batch, d), F32),
        ] + [_packed(w) for w in (w_in, w_cb, wa, wx, w_lb, w_out)],
        compiler_params=_COMPILER_PARAMS,
        name="mixer",
    )(x, *args, *ffn_weights)


def _convffn(x, batch, g_pre, g_post, w_up_packed, cw, cb, w_down_packed):
    n, d = x.shape
    s = n // batch
    d_ff = 2 * w_down_packed.shape[0]
    rows = STEPS_PER_TILE * batch
    return pl.pallas_call(
        _ffn_kernel,
        grid=(s // STEPS_PER_TILE,),
        in_specs=[pl.BlockSpec((rows, d), lambda j: (j, 0)),
                  _full(g_pre), _full(g_post), _full(w_up_packed), _full(cw), _full(cb),
                  _full(w_down_packed)],
        out_specs=pl.BlockSpec((batch, STEPS_PER_TILE, d), lambda j: (0, j, 0)),
        out_shape=jax.ShapeDtypeStruct((batch, s, d), x.dtype),
        scratch_shapes=[
            pltpu.VMEM((rows + (cw.shape[0] - 1) * batch, 2 * d_ff), F32),
            pltpu.VMEM((rows, d_ff), BF16),
        ],
        compiler_params=_COMPILER_PARAMS,
        name="convffn",
    )(x, g_pre, g_post, w_up_packed, cw, cb, w_down_packed)


def kernel(x, norm_mix_pre, norm_mix_post, norm_ffn_pre, norm_ffn_post, w_in, conv_short_w, w_conv_branch, lru_conv_w, lru_conv_b, lru_wa, lru_ba, lru_wx, lru_bx, lru_lambda, w_lru_branch, w_out, ffn_w_up, ffn_conv_w, ffn_conv_b, ffn_w_down):
    depth = w_in.shape[0]
    bsz, s, d = x.shape
    assert bsz == SUBLANES and s % STEPS_PER_TILE == 0 and STEPS_PER_TILE % STEPS_PER_CHUNK == 0
    assert ffn_w_down.shape[1] % FFN_CHUNK == 0
    for l in range(depth):
        xt, w_up_packed, w_down_packed = _mixer(
            x, norm_mix_pre[l][None], norm_mix_post[l][None], w_in[l],
            conv_short_w[l], w_conv_branch[l], lru_conv_w[l], lru_conv_b[l][None],
            lru_wa[l].reshape(d, -1), lru_ba[l].reshape(1, d),
            lru_wx[l].reshape(d, -1), lru_bx[l].reshape(1, d),
            lru_lambda[l][None], w_lru_branch[l], w_out[l], (ffn_w_up[l], ffn_w_down[l]))
        x = _convffn(
            xt, bsz, norm_ffn_pre[l][None], norm_ffn_post[l][None], w_up_packed,
            ffn_conv_w[l], ffn_conv_b[l][None], w_down_packed)
    return x
```

```python
import math

import jax
import jax.numpy as jnp
from jax import lax
from jax.experimental import pallas as pl
from jax.experimental.pallas import tpu as pltpu

F32 = jnp.float32
BF16 = jnp.bfloat16

LRU_HEADS = 4
LRU_C = 8.0
RMS_EPS = 1e-6
SUBLANES = 8
STEPS_PER_TILE = 64
STEPS_PER_CHUNK = 32
FFN_CHUNK = 256
PACK_STEPS = 8
VMEM_LIMIT_BYTES = 60 * 1024 * 1024


def _rmsnorm(x, g):
    ms = jnp.mean(x * x, axis=-1, keepdims=True)
    return x * lax.rsqrt(ms + RMS_EPS) * g


def _gelu_tanh(x):
    c = math.sqrt(2.0 / math.pi)
    return 0.5 * x * (1.0 + jnp.tanh(c * (x + 0.044715 * (x * x * x))))


def _log_sigmoid(x):
    return jnp.minimum(x, 0.0) - jnp.log1p(jnp.exp(-jnp.abs(x)))


def _pack_rows(step, src_ref, dst_ref):
    kb = src_ref.shape[0] // 2
    start = pl.multiple_of(step * kb, kb)
    dst_ref[pl.ds(start, kb), :] = pltpu.bitcast(src_ref[...].astype(BF16), jnp.uint32)


def _dot(a, w_packed):
    w = pltpu.bitcast(w_packed, BF16)
    return jnp.dot(a, w, preferred_element_type=F32)


def _causal_conv(buf, cur, w_ref, cols, row0, nrows, batch):
    k_width = w_ref.shape[0]
    acc = cur * w_ref[k_width - 1:k_width, cols]
    for k in range(k_width - 1):
        start = row0 + k * batch
        acc = acc + buf[start:start + nrows, cols] * w_ref[k:k + 1, cols]
    return acc


def _mixer_kernel(x_ref, g_pre_ref, g_post_ref, w_in_f32, csw_ref, w_cb_f32, lcw_ref, lcb_ref,
                  wa_f32, ba_ref, wx_f32, bx_ref, lam_ref, w_lb_f32, w_out_f32,
                  ffn_w_up_f32, ffn_w_down_f32,
                  o_ref, ffn_w_up_packed, ffn_w_down_packed, cv_buf, lx_buf, hs_buf, h_carry,
                  w_in_ref, w_cb_ref, wa_ref, wx_ref, w_lb_ref, w_out_ref):
    batch, steps, d = x_ref.shape
    rows = steps * batch
    crows = STEPS_PER_CHUNK * batch
    j = pl.program_id(0)
    cv_tail = cv_buf.shape[0] - rows
    lx_tail = lx_buf.shape[0] - rows
    all_cols = slice(0, d)
    dh = d // LRU_HEADS

    @pl.when(j == 0)
    def _reset_state():
        cv_buf[0:cv_tail, :] = jnp.zeros((cv_tail, d), F32)
        lx_buf[0:lx_tail, :] = jnp.zeros((lx_tail, d), F32)
        h_carry[...] = jnp.zeros_like(h_carry)

    @pl.when(j < PACK_STEPS)
    def _pack_weights():
        for src, dst in ((w_in_f32, w_in_ref), (w_cb_f32, w_cb_ref), (wa_f32, wa_ref),
                         (wx_f32, wx_ref), (w_lb_f32, w_lb_ref), (w_out_f32, w_out_ref)):
            _pack_rows(j, src, dst)

    @pl.when(j >= PACK_STEPS)
    def _tile():
        log_a_scale = LRU_C * _log_sigmoid(lam_ref[...])
        n_chunks = steps // STEPS_PER_CHUNK
        state = [{} for _ in range(n_chunks)]

        lru_state = {"h_t": h_carry[...]}

        def chunk_steps(c):
            s, r0, t0 = state[c], c * crows, c * STEPS_PER_CHUNK

            def prenorm():
                x = jnp.swapaxes(x_ref[:, t0:t0 + STEPS_PER_CHUNK, :], 0, 1).reshape(crows, d)
                s["x"] = x
                s["h"] = _rmsnorm(x, g_pre_ref[...]).astype(BF16)

            def proj(name, k):
                def run():
                    s[name] = _dot(s["h"], w_in_ref[:, k * d:(k + 1) * d])
                return run

            def lru_conv():
                l_x = s.pop("l_x")
                lx_buf[lx_tail + r0:lx_tail + r0 + crows, :] = l_x
                xl = _causal_conv(lx_buf, l_x, lcw_ref, all_cols, r0, crows, batch) + lcb_ref[...]
                s["xl"], s["xl_b"] = xl, xl.astype(BF16)

            def gate(name, w_ref, hd):
                def run():
                    s[name, hd] = _dot(s["xl_b"][:, hd * dh:(hd + 1) * dh],
                                       w_ref[hd * dh // 2:(hd + 1) * dh // 2, :])
                return run

            def conv_branch():
                v = s.pop("c_c") * s.pop("c_x")
                cv_buf[cv_tail + r0:cv_tail + r0 + crows, :] = v
                y_a = s.pop("c_b") * _causal_conv(cv_buf, v, csw_ref, all_cols, r0, crows, batch)
                s["br_a"] = _dot(y_a.astype(BF16), w_cb_ref[...])

            def recurrence():
                z_r = jnp.concatenate([s.pop(("z_r", hd)) for hd in range(LRU_HEADS)], axis=-1)
                z_i = jnp.concatenate([s.pop(("z_i", hd)) for hd in range(LRU_HEADS)], axis=-1)
                r = jax.nn.sigmoid(z_r + ba_ref[...])
                i = jax.nn.sigmoid(z_i + bx_ref[...])
                a = jnp.exp(r * log_a_scale)
                mult = jnp.sqrt(1.0 - a * a)
                if c == 0:
                    row = lax.broadcasted_iota(jnp.int32, (crows, d), 0)
                    seq_start_rows = jnp.where(j == PACK_STEPS, batch, 0)
                    mult = jnp.where(row < seq_start_rows, 1.0, mult)
                u = mult * (i * s.pop("xl"))
                h_t = lru_state["h_t"]
                for t in range(STEPS_PER_CHUNK):
                    rs = slice(t * batch, (t + 1) * batch)
                    h_t = a[rs, :] * h_t + u[rs, :]
                    hs_buf[r0 + t * batch:r0 + (t + 1) * batch, :] = h_t
                lru_state["h_t"] = h_t
                s["y_b"] = (hs_buf[r0:r0 + crows, :] * _gelu_tanh(s.pop("l_y"))).astype(BF16)

            def lru_branch_and_merge():
                br_b = _dot(s.pop("y_b"), w_lb_ref[...])
                merged = (jax.nn.sigmoid(s.pop("g_conv")) * s.pop("br_a")
                          + jax.nn.sigmoid(s.pop("g_lru")) * br_b)
                s["merged"] = merged.astype(BF16)

            def output():
                mix = _dot(s.pop("merged"), w_out_ref[...])
                o_ref[r0:r0 + crows, :] = s.pop("x") + _rmsnorm(mix, g_post_ref[...])

            head = [prenorm, proj("l_x", 3), proj("c_c", 1)]
            body = [lru_conv, gate("z_r", wa_ref, 0), proj("c_x", 2), gate("z_i", wx_ref, 0),
                    proj("c_b", 0), gate("z_r", wa_ref, 1), proj("l_y", 4), gate("z_i", wx_ref, 1),
                    proj("g_conv", 5), gate("z_r", wa_ref, 2), proj("g_lru", 6),
                    gate("z_i", wx_ref, 2), conv_branch, gate("z_r", wa_ref, 3),
                    gate("z_i", wx_ref, 3), recurrence]
            return head, body, lru_branch_and_merge, output

        plans = [chunk_steps(c) for c in range(n_chunks)]
        for step in plans[0][0]:
            step()
        for c in range(n_chunks):
            _, body, lru_branch_and_merge, output = plans[c]
            *front, gate_r3, gate_i3, recurrence = body
            start = 0 if c == 0 else 6
            for step in front[start:]:
                step()
            if c + 1 < n_chunks:
                prenorm, proj_lx, proj_cc = plans[c + 1][0]
                next_front = plans[c + 1][1][:6]
                prenorm()
                gate_r3()
                proj_lx()
                gate_i3()
                proj_cc()
                recurrence()
                for step in next_front[:2]:
                    step()
                lru_branch_and_merge()
                for step in next_front[2:]:
                    step()
                output()
            else:
                gate_r3()
                gate_i3()
                recurrence()
                lru_branch_and_merge()
                output()
        h_t = lru_state["h_t"]

        for src, dst in ((ffn_w_up_f32, ffn_w_up_packed), (ffn_w_down_f32, ffn_w_down_packed)):
            dst[...] = pltpu.bitcast(src[...].astype(BF16), jnp.uint32)
        h_carry[...] = h_t
        cv_buf[0:cv_tail, :] = cv_buf[rows:rows + cv_tail, :]
        lx_buf[0:lx_tail, :] = lx_buf[rows:rows + lx_tail, :]


def _ffn_kernel(x_ref, g_pre_ref, g_post_ref, w_up_ref, cw_ref, cb_ref, w_down_ref,
                o_ref, up_buf, f_buf):
    rows, d = x_ref.shape
    batch, steps, _ = o_ref.shape
    crows = STEPS_PER_CHUNK * batch
    d_ff = f_buf.shape[1]
    j = pl.program_id(0)
    tail = up_buf.shape[0] - rows

    @pl.when(j == 0)
    def _reset_state():
        up_buf[0:tail, :] = jnp.zeros((tail, up_buf.shape[1]), F32)

    for c in range(steps // STEPS_PER_CHUNK):
        r0 = c * crows
        t0 = c * STEPS_PER_CHUNK
        x = x_ref[r0:r0 + crows, :]
        h = _rmsnorm(x, g_pre_ref[...]).astype(BF16)

        def conv_up(col):
            cols = slice(col, col + FFN_CHUNK)
            up = _dot(h, w_up_ref[:, cols])
            up_buf[tail + r0:tail + r0 + crows, cols] = up
            return _causal_conv(up_buf, up, cw_ref, cols, r0, crows, batch) + cb_ref[:, cols]

        for n in range(d_ff // FFN_CHUNK):
            gate = conv_up(n * FFN_CHUNK)
            val = conv_up(d_ff + n * FFN_CHUNK)
            f_buf[r0:r0 + crows, n * FFN_CHUNK:(n + 1) * FFN_CHUNK] = (
                _gelu_tanh(gate) * val).astype(BF16)

        out = _dot(f_buf[r0:r0 + crows, :], w_down_ref[...])
        res = x + _rmsnorm(out, g_post_ref[...])
        o_ref[:, t0:t0 + STEPS_PER_CHUNK, :] = jnp.swapaxes(
            res.reshape(STEPS_PER_CHUNK, batch, d), 0, 1)

    up_buf[0:tail, :] = up_buf[rows:rows + tail, :]


def _full(a):
    return pl.BlockSpec(a.shape, lambda j: (0,) * a.ndim)


def _weight_rows(w):
    assert w.shape[0] % (PACK_STEPS * 2 * SUBLANES) == 0, w.shape
    return pl.BlockSpec((w.shape[0] // PACK_STEPS, w.shape[1]),
                        lambda j: (jnp.minimum(j, PACK_STEPS - 1), 0))


def _packed(w):
    return pltpu.VMEM((w.shape[0] // 2, w.shape[1]), jnp.uint32)


def _tile_index(j):
    return jnp.maximum(j - PACK_STEPS, 0)


_COMPILER_PARAMS = pltpu.CompilerParams(
    dimension_semantics=("arbitrary",), vmem_limit_bytes=VMEM_LIMIT_BYTES)


def _mixer(x, g_pre, g_post, w_in, csw, w_cb, lcw, lcb, wa, ba, wx, bx, lam, w_lb, w_out,
           ffn_weights):
    batch, s, d = x.shape
    rows = STEPS_PER_TILE * batch
    n_tiles = s // STEPS_PER_TILE
    args = (g_pre, g_post, w_in, csw, w_cb, lcw, lcb, wa, ba, wx, bx, lam, w_lb, w_out)

    def tile_rows(w, pack):
        assert w.shape[0] % (n_tiles * 2 * SUBLANES) == 0, (w.shape, n_tiles)
        return pl.BlockSpec((w.shape[0] // n_tiles // pack, w.shape[1]),
                            lambda j: (_tile_index(j), 0))

    return pl.pallas_call(
        _mixer_kernel,
        grid=(PACK_STEPS + n_tiles,),
        in_specs=[pl.BlockSpec((batch, STEPS_PER_TILE, d), lambda j: (0, _tile_index(j), 0)),
                  _full(g_pre), _full(g_post), _weight_rows(w_in), _full(csw), _weight_rows(w_cb),
                  _full(lcw), _full(lcb), _weight_rows(wa), _full(ba), _weight_rows(wx),
                  _full(bx), _full(lam), _weight_rows(w_lb), _weight_rows(w_out)]
        + [tile_rows(w, 1) for w in ffn_weights],
        out_specs=[pl.BlockSpec((rows, d), lambda j: (_tile_index(j), 0))]
        + [tile_rows(w, 2) for w in ffn_weights],
        out_shape=[jax.ShapeDtypeStruct((s * batch, d), x.dtype)]
        + [jax.ShapeDtypeStruct((w.shape[0] // 2, w.shape[1]), jnp.uint32) for w in ffn_weights],
        scratch_shapes=[
            pltpu.VMEM((rows + (csw.shape[0] - 1) * batch, d), F32),
            pltpu.VMEM((rows + (lcw.shape[0] - 1) * batch, d), F32),
            pltpu.VMEM((rows, d), F32),
            pltpu.VMEM((batch, d), F32),
        ] + [_packed(w) for w in (w_in, w_cb, wa, wx, w_lb, w_out)],
        compiler_params=_COMPILER_PARAMS,
        name="mixer",
    )(x, *args, *ffn_weights)


def _convffn(x, batch, g_pre, g_post, w_up_packed, cw, cb, w_down_packed):
    n, d = x.shape
    s = n // batch
    d_ff = 2 * w_down_packed.shape[0]
    rows = STEPS_PER_TILE * batch
    return pl.pallas_call(
        _ffn_kernel,
        grid=(s // STEPS_PER_TILE,),
        in_specs=[pl.BlockSpec((rows, d), lambda j: (j, 0)),
                  _full(g_pre), _full(g_post), _full(w_up_packed), _full(cw), _full(cb),
                  _full(w_down_packed)],
        out_specs=pl.BlockSpec((batch, STEPS_PER_TILE, d), lambda j: (0, j, 0)),
        out_shape=jax.ShapeDtypeStruct((batch, s, d), x.dtype),
        scratch_shapes=[
            pltpu.VMEM((rows + (cw.shape[0] - 1) * batch, 2 * d_ff), F32),
            pltpu.VMEM((rows, d_ff), BF16),
        ],
        compiler_params=_COMPILER_PARAMS,
        name="convffn",
    )(x, g_pre, g_post, w_up_packed, cw, cb, w_down_packed)


def kernel(x, norm_mix_pre, norm_mix_post, norm_ffn_pre, norm_ffn_post, w_in, conv_short_w, w_conv_branch, lru_conv_w, lru_conv_b, lru_wa, lru_ba, lru_wx, lru_bx, lru_lambda, w_lru_branch, w_out, ffn_w_up, ffn_conv_w, ffn_conv_b, ffn_w_down):
    depth = w_in.shape[0]
    bsz, s, d = x.shape
    assert bsz == SUBLANES and s % STEPS_PER_TILE == 0 and STEPS_PER_TILE % STEPS_PER_CHUNK == 0
    assert ffn_w_down.shape[1] % FFN_CHUNK == 0
    for l in range(depth):
        xt, w_up_packed, w_down_packed = _mixer(
            x, norm_mix_pre[l][None], norm_mix_post[l][None], w_in[l],
            conv_short_w[l], w_conv_branch[l], lru_conv_w[l], lru_conv_b[l][None],
            lru_wa[l].reshape(d, -1), lru_ba[l].reshape(1, d),
            lru_wx[l].reshape(d, -1), lru_bx[l].reshape(1, d),
            lru_lambda[l][None], w_lru_branch[l], w_out[l], (ffn_w_up[l], ffn_w_down[l]))
        x = _convffn(
            xt, bsz, norm_ffn_pre[l][None], norm_ffn_post[l][None], w_up_packed,
            ffn_conv_w[l], ffn_conv_b[l][None], w_down_packed)
    return x
```
